```python
import jax, jax.numpy as jnp
from jax import lax
import numpy as np

D_MODEL = 1024
BATCH = 2
SEQ = 8192
DEPTH = 2

N_MIXERS = 2
HEAD_DIM = 64
N_HEADS = D_MODEL // HEAD_DIM
DECAY_LORA = 64
ICLR_LORA = 64
GATE_LORA = 128
GN_EPS = 64e-5
N_DIR = 2
CHUNK = 128
GMLP_WIDTH = 2 * D_MODEL
GMLP_GROUP = 128
GMLP_GROUPS = GMLP_WIDTH // GMLP_GROUP
FFN_HIDDEN = 4 * D_MODEL
NORM_EPS = 1e-5
N_RWKV = (DEPTH + 1) // 2
N_GMLP = DEPTH // 2

kernel_name = "bidir_rwkv7_chunked_gmlp_hybrid"


def rms_norm(x, g):
    xf = x.astype(jnp.float32)
    y = xf * lax.rsqrt(jnp.mean(xf * xf, axis=-1, keepdims=True) + NORM_EPS)
    return (y * g.astype(jnp.float32)).astype(x.dtype)


def layer_norm(x, g, b):
    xf = x.astype(jnp.float32)
    mu = jnp.mean(xf, axis=-1, keepdims=True)
    xc = xf - mu
    var = jnp.mean(xc * xc, axis=-1, keepdims=True)
    y = xc * lax.rsqrt(var + NORM_EPS) * g.astype(jnp.float32) + b.astype(jnp.float32)
    return y.astype(x.dtype)


def centred_shift(x):
    x_prev = jnp.pad(x[:, :-1], ((0, 0), (1, 0), (0, 0)))
    x_next = jnp.pad(x[:, 1:], ((0, 0), (0, 1), (0, 0)))
    return 0.5 * (x_prev + x_next) - x


def wkv7_scan(r, w, k, v, a_vec, b_vec, reverse):
    B, T, H, N = r.shape
    xs = tuple(jnp.moveaxis(t, 1, 0) for t in (r, w, k, v, a_vec, b_vec))

    def step(S, inp):
        r_t, w_t, k_t, v_t, a_t, b_t = inp
        sa = jnp.einsum('bhvk,bhk->bhv', S, a_t)
        S = S * w_t[:, :, None, :] + sa[..., None] * b_t[:, :, None, :] + v_t[..., None] * k_t[:, :, None, :]
        o = jnp.einsum('bhvk,bhk->bhv', S, r_t)
        return S, o

    S0 = jnp.zeros((B, H, N, N), jnp.float32)
    _, o = lax.scan(step, S0, xs, reverse=reverse)
    return jnp.moveaxis(o, 0, 1)


def rwkv7_bidir(xn, mu, wr, wk, wv, wo, w0, w1, w2, a0, a1, a2, g1, g2, k_k, k_a, r_k, ln_g, ln_b):
    B, T, D = xn.shape
    heads = lambda t: t.reshape(B, T, N_HEADS, HEAD_DIM)
    xx = centred_shift(xn)
    xr, xw, xk, xv, xa, xg = (xn + xx * mu[i] for i in range(6))
    r = heads(xr @ wr).astype(jnp.float32)
    k = heads(xk @ wk).astype(jnp.float32)
    v = heads(xv @ wv).astype(jnp.float32)
    g = jax.nn.sigmoid(xg @ g1) @ g2
    kk = k * k_k.reshape(N_HEADS, HEAD_DIM).astype(jnp.float32)
    kk = kk / jnp.maximum(jnp.sqrt(jnp.sum(kk * kk, axis=-1, keepdims=True)), 1e-12)
    k_a_h = k_a.reshape(N_HEADS, HEAD_DIM).astype(jnp.float32)
    r_k_f = r_k.astype(jnp.float32)
    scan_sum = jnp.zeros_like(r)
    bonus_sum = jnp.zeros_like(r)
    for d in range(N_DIR):
        w_log = -jax.nn.softplus(-(w0[d] + jnp.tanh(xw @ w1[d]) @ w2[d]).astype(jnp.float32)) - 0.5
        decay = heads(jnp.exp(-jnp.exp(w_log)))
        a = heads(jax.nn.sigmoid((a0[d] + (xa @ a1[d]) @ a2[d]).astype(jnp.float32)))
        k_d = k * (1.0 + (a - 1.0) * k_a_h)
        scan_sum = scan_sum + wkv7_scan(r, decay, k_d, v, -kk, kk * a, reverse=(d == 1))
        bonus_sum = bonus_sum + jnp.sum(r * k_d * r_k_f, axis=-1, keepdims=True) * v
    mu_h = jnp.mean(scan_sum, axis=-1, keepdims=True)
    oc = scan_sum - mu_h
    o = oc * lax.rsqrt(jnp.mean(oc * oc, axis=-1, keepdims=True) + GN_EPS)
    o = o.reshape(B, T, D) * ln_g.astype(jnp.float32) + ln_b.astype(jnp.float32)
    o = o + bonus_sum.reshape(B, T, D)
    return (o.astype(xn.dtype) * g) @ wo


def chunked_gmlp(xn, w_in, ln_g, ln_b, w_s, b_s, w_out):
    B, T, D = xn.shape
    h = jax.nn.gelu(xn @ w_in, approximate=False)
    u, v = jnp.split(h, 2, axis=-1)
    v = layer_norm(v, ln_g, ln_b)
    v = v.reshape(B, T // CHUNK, CHUNK, GMLP_GROUPS, GMLP_GROUP)
    v = jnp.einsum('gij,bcjgd->bcigd', w_s, v) + jnp.transpose(b_s)[None, None, :, :, None]
    v = v.reshape(B, T, GMLP_WIDTH)
    return (u * v) @ w_out


def sqrelu_ffn(xn, w1, w2):
    h = jax.nn.relu(xn @ w1)
    return (h * h) @ w2


def setup_inputs(seed: int = 0) -> dict:
    key = jax.random.key(seed)
    ks = iter(jax.random.split(key, 40))
    nrm = lambda shape, scale: jax.random.normal(next(ks), shape, jnp.float32) * scale
    D, NR, NG = D_MODEL, N_RWKV, N_GMLP
    return {
        "x": nrm((BATCH, SEQ, D), 1.0),
        "mix_norm": 1.0 + nrm((DEPTH, D), 0.02),
        "ffn_norm": 1.0 + nrm((DEPTH, D), 0.02),
        "final_norm": 1.0 + nrm((D,), 0.02),
        "rwkv_mu": jax.random.uniform(next(ks), (NR, 6, D), jnp.float32),
        "rwkv_wr": nrm((NR, D, D), D ** -0.5),
        "rwkv_wk": nrm((NR, D, D), D ** -0.5),
        "rwkv_wv": nrm((NR, D, D), D ** -0.5),
        "rwkv_wo": nrm((NR, D, D), D ** -0.5),
        "rwkv_w0": jax.random.uniform(next(ks), (NR, N_DIR, D), jnp.float32, -6.0, 1.0),
        "rwkv_w1": nrm((NR, N_DIR, D, DECAY_LORA), D ** -0.5),
        "rwkv_w2": nrm((NR, N_DIR, DECAY_LORA, D), 0.1 * DECAY_LORA ** -0.5),
        "rwkv_a0": nrm((NR, N_DIR, D), 0.1),
        "rwkv_a1": nrm((NR, N_DIR, D, ICLR_LORA), D ** -0.5),
        "rwkv_a2": nrm((NR, N_DIR, ICLR_LORA, D), 0.1 * ICLR_LORA ** -0.5),
        "rwkv_g1": nrm((NR, D, GATE_LORA), D ** -0.5),
        "rwkv_g2": nrm((NR, GATE_LORA, D), GATE_LORA ** -0.5),
        "rwkv_k_k": 0.85 + nrm((NR, D), 0.02),
        "rwkv_k_a": 1.0 + nrm((NR, D), 0.02),
        "rwkv_r_k": nrm((NR, N_HEADS, HEAD_DIM), 0.1),
        "rwkv_ln_g": 1.0 + nrm((NR, D), 0.02),
        "rwkv_ln_b": nrm((NR, D), 0.02),
        "gmlp_w_in": nrm((NG, D, 2 * GMLP_WIDTH), D ** -0.5),
        "gmlp_ln_g": 1.0 + nrm((NG, GMLP_WIDTH), 0.02),
        "gmlp_ln_b": nrm((NG, GMLP_WIDTH), 0.02),
        "gmlp_w_s": nrm((NG, GMLP_GROUPS, CHUNK, CHUNK), CHUNK ** -0.5),
        "gmlp_b_s": 1.0 + nrm((NG, GMLP_GROUPS, CHUNK), 0.1),
        "gmlp_w_out": nrm((NG, GMLP_WIDTH, D), GMLP_WIDTH ** -0.5),
        "ffn_w1": nrm((DEPTH, D, FFN_HIDDEN), D ** -0.5),
        "ffn_w2": nrm((DEPTH, FFN_HIDDEN, D), FFN_HIDDEN ** -0.5),
    }


def reference(x, mix_norm, ffn_norm, final_norm, rwkv_mu, rwkv_wr, rwkv_wk, rwkv_wv, rwkv_wo,
              rwkv_w0, rwkv_w1, rwkv_w2, rwkv_a0, rwkv_a1, rwkv_a2, rwkv_g1, rwkv_g2,
              rwkv_k_k, rwkv_k_a, rwkv_r_k, rwkv_ln_g, rwkv_ln_b,
              gmlp_w_in, gmlp_ln_g, gmlp_ln_b, gmlp_w_s, gmlp_b_s, gmlp_w_out,
              ffn_w1, ffn_w2):
    for i in range(DEPTH):
        xn = rms_norm(x, mix_norm[i])
        j = i // N_MIXERS
        if i % N_MIXERS == 0:
            h = rwkv7_bidir(xn, rwkv_mu[j], rwkv_wr[j], rwkv_wk[j], rwkv_wv[j], rwkv_wo[j],
                            rwkv_w0[j], rwkv_w1[j], rwkv_w2[j], rwkv_a0[j], rwkv_a1[j], rwkv_a2[j],
                            rwkv_g1[j], rwkv_g2[j], rwkv_k_k[j], rwkv_k_a[j], rwkv_r_k[j],
                            rwkv_ln_g[j], rwkv_ln_b[j])
        else:
            h = chunked_gmlp(xn, gmlp_w_in[j], gmlp_ln_g[j], gmlp_ln_b[j], gmlp_w_s[j],
                             gmlp_b_s[j], gmlp_w_out[j])
        x = x + h
        x = x + sqrelu_ffn(rms_norm(x, ffn_norm[i]), ffn_w1[i], ffn_w2[i])
    return rms_norm(x, final_norm)
```

```python
import functools
import math

import jax
import jax.numpy as jnp
from jax import lax
from jax.experimental import pallas as pl
from jax.experimental.pallas import tpu as pltpu

F32 = jnp.float32
BF16 = jnp.bfloat16

HEAD_DIM = 64
GN_EPS = 64e-5
NORM_EPS = 1e-5
GMLP_CHUNK = 128
GMLP_GROUP = 128

LANES = 128
SUBLANES = 8
HEADS_PER_TILE = LANES // HEAD_DIM
VMEM_LIMIT_BYTES = 56 * 1024 * 1024

SCAN_CHUNK = 64
PAIR = HEADS_PER_TILE * SCAN_CHUNK


def _mm(a, b):
    return jnp.dot(a.astype(BF16), b.astype(BF16), preferred_element_type=F32)


def _mm_nt(a, b):
    return lax.dot_general(a.astype(BF16), b.astype(BF16), (((1,), (1,)), ((), ())),
                           preferred_element_type=F32)


def _mm_tn(a, b):
    return lax.dot_general(a.astype(BF16), b.astype(BF16), (((0,), (0,)), ((), ())),
                           preferred_element_type=F32)


def _split3(x):
    hi = x.astype(BF16)
    rem = x - hi.astype(F32)
    mid = rem.astype(BF16)
    lo = (rem - mid.astype(F32)).astype(BF16)
    return hi, mid, lo


def _rms(x, g):
    return x * lax.rsqrt(jnp.mean(x * x, axis=-1, keepdims=True) + NORM_EPS) * g


def _head_sum(x, ones_bd):
    hi = x.astype(BF16)
    lo = (x - hi.astype(F32)).astype(BF16)
    cols = []
    for j in range(x.shape[1] // LANES):
        sl = slice(j * LANES, (j + 1) * LANES)
        cols.append(jnp.dot(hi[:, sl], ones_bd, preferred_element_type=F32)
                    + jnp.dot(lo[:, sl], ones_bd, preferred_element_type=F32))
    return jnp.concatenate(cols, axis=1)


def _rwkv_pre_kernel(x_ref, xprev_ref, xnext_ref, norm_ref, mu_ref,
                     wr_ref, wk_ref, wv_ref, g1_ref, g2_ref,
                     w1_ref, w2_ref, w0_ref, a1_ref, a2_ref, a0_ref,
                     kscale_ref, ka_ref, rk_ref, ones_ref,
                     r_out, v_out, kk_out, g_out, bonus_out,
                     b0_out, kd0_out, lw0_out, b1_out, kd1_out, lw1_out,
                     *, tm, seq_len):
    i = pl.program_id(0)
    d = x_ref.shape[1]
    gain = norm_ref[...]
    xn = _rms(x_ref[...], gain)
    at_start = (i * tm) % seq_len == 0
    at_end = ((i + 1) * tm) % seq_len == 0
    edge_prev = jnp.where(at_start, 0.0, _rms(xprev_ref[SUBLANES - 1:SUBLANES, :], gain))
    edge_next = jnp.where(at_end, 0.0, _rms(xnext_ref[0:1, :], gain))
    row = lax.broadcasted_iota(jnp.int32, (tm, 1), 0)
    x_prev = jnp.where(row == 0, edge_prev, pltpu.roll(xn, 1, axis=0))
    x_next = jnp.where(row == tm - 1, edge_next, pltpu.roll(xn, tm - 1, axis=0))
    xx = 0.5 * (x_prev + x_next) - xn

    def mixed(j):
        return (xn + xx * mu_ref[j:j + 1, :]).astype(BF16)

    r = jnp.dot(mixed(0), wr_ref[...], preferred_element_type=F32)
    zw = _mm(jnp.tanh(jnp.dot(mixed(1), w1_ref[...], preferred_element_type=F32)),
             w2_ref[...]) + w0_ref[...]
    k = jnp.dot(mixed(2), wk_ref[...], preferred_element_type=F32)
    v = jnp.dot(mixed(3), wv_ref[...], preferred_element_type=F32)
    za = _mm(jnp.dot(mixed(4), a1_ref[...], preferred_element_type=F32),
             a2_ref[...]) + a0_ref[...]
    g_out[...] = _mm(jax.nn.sigmoid(jnp.dot(mixed(5), g1_ref[...], preferred_element_type=F32)),
                     g2_ref[...])
    r_out[...] = r
    v_out[...] = v

    ones_bd = ones_ref[...]
    kk = k * kscale_ref[...]
    kk = kk / jnp.maximum(jnp.sqrt(_head_sum(kk * kk, ones_bd)), 1e-12)
    kk_out[...] = kk

    k_a = ka_ref[...]
    kd_sum = jnp.zeros_like(k)
    for direction, (b_out, kd_out, lw_out) in enumerate(
            ((b0_out, kd0_out, lw0_out), (b1_out, kd1_out, lw1_out))):
        sl = slice(direction * d, (direction + 1) * d)
        lw_out[...] = (-math.exp(-0.5)) * jax.nn.sigmoid(zw[:, sl])
        a_gate = jax.nn.sigmoid(za[:, sl])
        kd = k * (1.0 + (a_gate - 1.0) * k_a)
        kd_out[...] = kd
        b_out[...] = kk * a_gate
        kd_sum = kd_sum + kd
    bonus_out[...] = _head_sum(r * kd_sum * rk_ref[...], ones_bd) * v


def _rwkv_pre(x2d, seq_len, norm, mu, wr, wk, wv, g1, g2, w1, w2, w0, a1, a2, a0,
              k_scale, k_a, r_k, ones_bd, *, tm):
    m, d = x2d.shape
    n_halo = m // SUBLANES
    halo = tm // SUBLANES
    row_spec = pl.BlockSpec((tm, d), lambda i: (i, 0))

    def full(a):
        return pl.BlockSpec(a.shape, lambda i: (0,) * a.ndim)

    params = (norm, mu, wr, wk, wv, g1, g2, w1, w2, w0, a1, a2, a0, k_scale, k_a, r_k, ones_bd)
    return pl.pallas_call(
        functools.partial(_rwkv_pre_kernel, tm=tm, seq_len=seq_len),
        grid=(m // tm,),
        in_specs=[row_spec,
                  pl.BlockSpec((SUBLANES, d), lambda i: (jnp.maximum(i * halo - 1, 0), 0)),
                  pl.BlockSpec((SUBLANES, d), lambda i: (jnp.minimum((i + 1) * halo, n_halo - 1), 0)),
                  ] + [full(p) for p in params],
        out_specs=[row_spec] * 11,
        out_shape=[jax.ShapeDtypeStruct((m, d), F32)] * 11,
        compiler_params=pltpu.CompilerParams(
            dimension_semantics=("arbitrary",), vmem_limit_bytes=VMEM_LIMIT_BYTES),
        name="rwkv_pre",
    )(x2d, x2d, x2d, *params)


def _wkv_kernel(r_ref, v_ref, kk_ref, b_ref, kd_ref, lw_ref, o_ref, state_ref, *, reverse):
    @pl.when(pl.program_id(1) == 0)
    def _():
        state_ref[...] = jnp.zeros_like(state_ref)

    n = SCAN_CHUNK
    d = r_ref.shape[1]
    ti = lax.broadcasted_iota(jnp.int32, (n, n), 0)
    tj = lax.broadcasted_iota(jnp.int32, (n, n), 1)
    cum_mat = ((ti <= tj) if reverse else (ti >= tj)).astype(BF16)
    pi = lax.broadcasted_iota(jnp.int32, (PAIR, PAIR), 0)
    pj = lax.broadcasted_iota(jnp.int32, (PAIR, PAIR), 1)
    strict = (pi < pj) if reverse else (pi > pj)
    incl = (pi <= pj) if reverse else (pi >= pj)
    head0 = lax.broadcasted_iota(jnp.int32, (n, LANES), 1) < HEAD_DIM

    lw = lw_ref[...]
    hi, mid, lo = _split3(lw)
    cum = (jnp.dot(cum_mat, hi, preferred_element_type=F32)
           + jnp.dot(cum_mat, mid, preferred_element_type=F32)
           + jnp.dot(cum_mat, lo, preferred_element_type=F32))
    e_incl = jnp.exp(cum)
    e_excl = jnp.exp(cum - lw)
    e_inv = jnp.exp(-cum)
    a_t = -(kk_ref[...] * e_excl)
    r_t = r_ref[...] * e_incl
    b_t = b_ref[...] * e_inv
    k_t = kd_ref[...] * e_inv
    last = 0 if reverse else n - 1
    chunk_decay = e_incl[last:last + 1, :]

    def expand(x):
        return jnp.concatenate([jnp.where(head0, x, 0.0), jnp.where(head0, 0.0, x)], axis=0)

    for p in range(d // LANES):
        sl = slice(p * LANES, (p + 1) * LANES)
        ar = jnp.concatenate([expand(a_t[:, sl]), expand(r_t[:, sl])], axis=0).astype(BF16)
        bk = jnp.concatenate([expand(b_t[:, sl]), expand(k_t[:, sl])], axis=0).astype(BF16)
        v2 = expand(v_ref[:, sl]).astype(BF16)
        state = state_ref[p]

        gram = _mm_nt(ar, bk)
        n_ab = jnp.where(strict, gram[:PAIR, :PAIR], 0.0)
        a_ak = jnp.where(strict, gram[:PAIR, PAIR:], 0.0)
        a_rb = jnp.where(incl, gram[PAIR:, :PAIR], 0.0)
        a_rk = jnp.where(incl, gram[PAIR:, PAIR:], 0.0)
        from_state = _mm_nt(ar, state)

        u = from_state[:PAIR] + _mm(a_ak, v2)
        power = n_ab
        steps = int(math.log2(n))
        for s in range(steps):
            u = u + _mm(power, u)
            if s + 1 < steps:
                power = _mm(power, power)

        uv = jnp.concatenate([u.astype(BF16), v2], axis=0)
        out2 = from_state[PAIR:] + _mm(jnp.concatenate([a_rb, a_rk], axis=1), uv)
        o_ref[:, sl] = out2[:n] + out2[n:]
        state_ref[p] = (state + _mm_tn(uv, bk)) * chunk_decay[:, sl]


def _wkv_scan(r, v, kk, b, kd, lw, batch, seq_len, *, reverse):
    m, d = r.shape
    n_chunks = seq_len // SCAN_CHUNK

    def idx(bi, c):
        c = (n_chunks - 1 - c) if reverse else c
        return (bi * n_chunks + c, 0)

    spec = pl.BlockSpec((SCAN_CHUNK, d), idx)
    return pl.pallas_call(
        functools.partial(_wkv_kernel, reverse=reverse),
        grid=(batch, n_chunks),
        in_specs=[spec] * 6,
        out_specs=spec,
        out_shape=jax.ShapeDtypeStruct((m, d), F32),
        scratch_shapes=[pltpu.VMEM((d // LANES, PAIR, LANES), F32)],
        compiler_params=pltpu.CompilerParams(
            dimension_semantics=("arbitrary", "arbitrary"), vmem_limit_bytes=VMEM_LIMIT_BYTES),
        name="wkv_rev" if reverse else "wkv_fwd",
    )(r, v, kk, b, kd, lw)


def _rwkv_post_kernel(x_ref, s0_ref, s1_ref, bonus_ref, g_ref, lng_ref, lnb_ref, ones_ref,
                      wo_ref, o_ref):
    ones_bd = ones_ref[...]
    s = s0_ref[...] + s1_ref[...]
    oc = s - _head_sum(s, ones_bd) * (1.0 / HEAD_DIM)
    var = _head_sum(oc * oc, ones_bd) * (1.0 / HEAD_DIM)
    o = oc * lax.rsqrt(var + GN_EPS) * lng_ref[...] + lnb_ref[...] + bonus_ref[...]
    o_ref[...] = x_ref[...] + jnp.dot((o * g_ref[...]).astype(BF16), wo_ref[...],
                                      preferred_element_type=F32)


def _rwkv_post(x2d, s0, s1, bonus, g, ln_g, ln_b, ones_bd, wo, *, tm):
    m, d = x2d.shape
    row_spec = pl.BlockSpec((tm, d), lambda i: (i, 0))

    def full(a):
        return pl.BlockSpec(a.shape, lambda i: (0,) * a.ndim)

    return pl.pallas_call(
        _rwkv_post_kernel,
        grid=(m // tm,),
        in_specs=[row_spec] * 5 + [full(ln_g), full(ln_b), full(ones_bd), full(wo)],
        out_specs=row_spec,
        out_shape=jax.ShapeDtypeStruct((m, d), F32),
        compiler_params=pltpu.CompilerParams(
            dimension_semantics=("arbitrary",), vmem_limit_bytes=VMEM_LIMIT_BYTES),
        name="rwkv_post",
    )(x2d, s0, s1, bonus, g, ln_g, ln_b, ones_bd, wo)


def _ffn_kernel(x_ref, norm_ref, w1_ref, w2_ref, fnorm_ref, o_ref, xn_ref, acc_ref, *, final_norm):
    j = pl.program_id(1)

    @pl.when(j == 0)
    def _():
        xn_ref[...] = _rms(x_ref[...], norm_ref[...]).astype(BF16)
        acc_ref[...] = jnp.zeros_like(acc_ref)

    h = jnp.maximum(jnp.dot(xn_ref[...], w1_ref[...], preferred_element_type=F32), 0.0)
    acc_ref[...] += jnp.dot((h * h).astype(BF16), w2_ref[...], preferred_element_type=F32)

    @pl.when(j == pl.num_programs(1) - 1)
    def _():
        y = x_ref[...] + acc_ref[...]
        o_ref[...] = _rms(y, fnorm_ref[...]) if final_norm else y


def _ffn(x2d, norm, w1, w2, fnorm, *, tm, th, final_norm):
    m, d = x2d.shape
    hidden = w1.shape[1]
    return pl.pallas_call(
        functools.partial(_ffn_kernel, final_norm=final_norm),
        grid=(m // tm, hidden // th),
        in_specs=[pl.BlockSpec((tm, d), lambda i, j: (i, 0)),
                  pl.BlockSpec((1, d), lambda i, j: (0, 0)),
                  pl.BlockSpec((d, th), lambda i, j: (0, j)),
                  pl.BlockSpec((th, d), lambda i, j: (j, 0)),
                  pl.BlockSpec((1, d), lambda i, j: (0, 0))],
        out_specs=pl.BlockSpec((tm, d), lambda i, j: (i, 0)),
        out_shape=jax.ShapeDtypeStruct((m, d), F32),
        scratch_shapes=[pltpu.VMEM((tm, d), BF16), pltpu.VMEM((tm, d), F32)],
        compiler_params=pltpu.CompilerParams(
            dimension_semantics=("arbitrary", "arbitrary"), vmem_limit_bytes=VMEM_LIMIT_BYTES),
        name="ffn_final" if final_norm else "ffn",
    )(x2d, norm, w1, w2, fnorm)


def _gmlp_kernel(x_ref, norm_ref, win_ref, lng_ref, lnb_ref, ws_ref, bs_ref, wout_ref,
                 o_ref, gated_ref, *, tm):
    x = x_ref[...]
    xn = _rms(x, norm_ref[...]).astype(BF16)
    h = jnp.dot(xn, win_ref[...], preferred_element_type=F32)
    h = 0.5 * h * (1.0 + lax.erf(h * (1.0 / math.sqrt(2.0))))
    width = h.shape[1] // 2
    u = h[:, :width]
    v = h[:, width:]
    mean = jnp.mean(v, axis=-1, keepdims=True)
    vc = v - mean
    var = jnp.mean(vc * vc, axis=-1, keepdims=True)
    v = (vc * lax.rsqrt(var + NORM_EPS) * lng_ref[...] + lnb_ref[...]).astype(BF16)
    for c in range(tm // GMLP_CHUNK):
        rows = slice(c * GMLP_CHUNK, (c + 1) * GMLP_CHUNK)
        for grp in range(width // GMLP_GROUP):
            cols = slice(grp * GMLP_GROUP, (grp + 1) * GMLP_GROUP)
            mixed = jnp.dot(ws_ref[grp], v[rows, cols], preferred_element_type=F32) + bs_ref[:, cols]
            gated_ref[rows, cols] = (u[rows, cols] * mixed).astype(BF16)
    o_ref[...] = x + jnp.dot(gated_ref[...], wout_ref[...], preferred_element_type=F32)


def _gmlp(x2d, norm, w_in, ln_g, ln_b, w_s, b_s_full, w_out, *, tm):
    m, d = x2d.shape
    width = w_out.shape[0]
    row_spec = pl.BlockSpec((tm, d), lambda i: (i, 0))

    def full(a):
        return pl.BlockSpec(a.shape, lambda i: (0,) * a.ndim)

    params = (norm, w_in, ln_g, ln_b, w_s, b_s_full, w_out)
    return pl.pallas_call(
        functools.partial(_gmlp_kernel, tm=tm),
        grid=(m // tm,),
        in_specs=[row_spec] + [full(p) for p in params],
        out_specs=row_spec,
        out_shape=jax.ShapeDtypeStruct((m, d), F32),
        scratch_shapes=[pltpu.VMEM((tm, width), BF16)],
        compiler_params=pltpu.CompilerParams(
            dimension_semantics=("arbitrary",), vmem_limit_bytes=VMEM_LIMIT_BYTES),
        name="gmlp",
    )(x2d, *params)


def _block_diag2(w):
    z = jnp.zeros_like(w[0])
    return jnp.concatenate([jnp.concatenate([w[0], z], axis=1),
                            jnp.concatenate([z, w[1]], axis=1)], axis=0)


def kernel(x, mix_norm, ffn_norm, final_norm, rwkv_mu, rwkv_wr, rwkv_wk, rwkv_wv, rwkv_wo, rwkv_w0, rwkv_w1, rwkv_w2, rwkv_a0, rwkv_a1, rwkv_a2, rwkv_g1, rwkv_g2, rwkv_k_k, rwkv_k_a, rwkv_r_k, rwkv_ln_g, rwkv_ln_b, gmlp_w_in, gmlp_ln_g, gmlp_ln_b, gmlp_w_s, gmlp_b_s, gmlp_w_out, ffn_w1, ffn_w2):
    batch, seq_len, d = x.shape
    depth = mix_norm.shape[0]
    m = batch * seq_len
    assert d % LANES == 0 and seq_len % SCAN_CHUNK == 0 and seq_len % GMLP_CHUNK == 0
    tm_pre = min(256, seq_len)
    tm_post = min(512, seq_len)
    tm_ffn = min(1024, m)
    tm_gmlp = min(256, seq_len)
    th_ffn = 1024

    lane = jnp.arange(LANES) // HEAD_DIM
    ones_bd = (lane[:, None] == lane[None, :]).astype(BF16)
    row = lambda a: a.reshape(1, -1).astype(F32)
    cat_dirs = lambda w: jnp.concatenate([w[0], w[1]], axis=-1)

    h = x.reshape(m, d)
    for layer in range(depth):
        j = layer // 2
        norm = row(mix_norm[layer])
        if layer % 2 == 0:
            outs = _rwkv_pre(
                h, seq_len, norm, rwkv_mu[j],
                rwkv_wr[j].astype(BF16), rwkv_wk[j].astype(BF16), rwkv_wv[j].astype(BF16),
                rwkv_g1[j].astype(BF16), rwkv_g2[j].astype(BF16),
                cat_dirs(rwkv_w1[j]).astype(BF16), _block_diag2(rwkv_w2[j]).astype(BF16),
                row(cat_dirs(rwkv_w0[j])),
                cat_dirs(rwkv_a1[j]).astype(BF16), _block_diag2(rwkv_a2[j]).astype(BF16),
                row(cat_dirs(rwkv_a0[j])),
                row(rwkv_k_k[j]), row(rwkv_k_a[j]), row(rwkv_r_k[j]), ones_bd, tm=tm_pre)
            r, v, kk, g, bonus, b0, kd0, lw0, b1, kd1, lw1 = outs
            s0 = _wkv_scan(r, v, kk, b0, kd0, lw0, batch, seq_len, reverse=False)
            s1 = _wkv_scan(r, v, kk, b1, kd1, lw1, batch, seq_len, reverse=True)
            h = _rwkv_post(h, s0, s1, bonus, g, row(rwkv_ln_g[j]), row(rwkv_ln_b[j]), ones_bd,
                           rwkv_wo[j].astype(BF16), tm=tm_post)
        else:
            b_full = jnp.repeat(jnp.transpose(gmlp_b_s[j]), GMLP_GROUP, axis=1)
            h = _gmlp(h, norm, gmlp_w_in[j].astype(BF16), row(gmlp_ln_g[j]), row(gmlp_ln_b[j]),
                      gmlp_w_s[j].astype(BF16), b_full, gmlp_w_out[j].astype(BF16), tm=tm_gmlp)
        last = layer == depth - 1
        h = _ffn(h, row(ffn_norm[layer]), ffn_w1[layer].astype(BF16), ffn_w2[layer].astype(BF16),
                 row(final_norm), tm=tm_ffn, th=th_ffn, final_norm=last)
    return h.reshape(batch, seq_len, d)
```

```python
import functools
import math

import jax
import jax.numpy as jnp
from jax import lax
from jax.experimental import pallas as pl
from jax.experimental.pallas import tpu as pltpu

F32 = jnp.float32
BF16 = jnp.bfloat16

HEAD_DIM = 64
GN_EPS = 64e-5
NORM_EPS = 1e-5
GMLP_CHUNK = 128
GMLP_GROUP = 128

LANES = 128
SUBLANES = 8
HEADS_PER_TILE = LANES // HEAD_DIM
VMEM_LIMIT_BYTES = 56 * 1024 * 1024

SCAN_CHUNK = 64
PAIR = HEADS_PER_TILE * SCAN_CHUNK


def _mm(a, b):
    return jnp.dot(a.astype(BF16), b.astype(BF16), preferred_element_type=F32)


def _mm_nt(a, b):
    return lax.dot_general(a.astype(BF16), b.astype(BF16), (((1,), (1,)), ((), ())),
                           preferred_element_type=F32)


def _mm_tn(a, b):
    return lax.dot_general(a.astype(BF16), b.astype(BF16), (((0,), (0,)), ((), ())),
                           preferred_element_type=F32)


def _split3(x):
    hi = x.astype(BF16)
    rem = x - hi.astype(F32)
    mid = rem.astype(BF16)
    lo = (rem - mid.astype(F32)).astype(BF16)
    return hi, mid, lo


def _rms(x, g):
    return x * lax.rsqrt(jnp.mean(x * x, axis=-1, keepdims=True) + NORM_EPS) * g


def _head_sum(x, ones_bd):
    hi = x.astype(BF16)
    lo = (x - hi.astype(F32)).astype(BF16)
    cols = []
    for j in range(x.shape[1] // LANES):
        sl = slice(j * LANES, (j + 1) * LANES)
        cols.append(jnp.dot(hi[:, sl], ones_bd, preferred_element_type=F32)
                    + jnp.dot(lo[:, sl], ones_bd, preferred_element_type=F32))
    return jnp.concatenate(cols, axis=1)


def _rwkv_pre_kernel(x_ref, xprev_ref, xnext_ref, norm_ref, mu_ref,
                     wr_ref, wk_ref, wv_ref, g1_ref, g2_ref,
                     w1_ref, w2_ref, w0_ref, a1_ref, a2_ref, a0_ref,
                     kscale_ref, ka_ref, rk_ref, ones_ref,
                     r_out, v_out, kk_out, g_out, bonus_out,
                     b0_out, kd0_out, lw0_out, b1_out, kd1_out, lw1_out,
                     *, tm, seq_len):
    i = pl.program_id(0)
    d = x_ref.shape[1]
    gain = norm_ref[...]
    xn = _rms(x_ref[...], gain)
    at_start = (i * tm) % seq_len == 0
    at_end = ((i + 1) * tm) % seq_len == 0
    edge_prev = jnp.where(at_start, 0.0, _rms(xprev_ref[SUBLANES - 1:SUBLANES, :], gain))
    edge_next = jnp.where(at_end, 0.0, _rms(xnext_ref[0:1, :], gain))
    row = lax.broadcasted_iota(jnp.int32, (tm, 1), 0)
    x_prev = jnp.where(row == 0, edge_prev, pltpu.roll(xn, 1, axis=0))
    x_next = jnp.where(row == tm - 1, edge_next, pltpu.roll(xn, tm - 1, axis=0))
    xx = 0.5 * (x_prev + x_next) - xn

    def mixed(j):
        return (xn + xx * mu_ref[j:j + 1, :]).astype(BF16)

    r = jnp.dot(mixed(0), wr_ref[...], preferred_element_type=F32)
    zw = _mm(jnp.tanh(jnp.dot(mixed(1), w1_ref[...], preferred_element_type=F32)),
             w2_ref[...]) + w0_ref[...]
    k = jnp.dot(mixed(2), wk_ref[...], preferred_element_type=F32)
    v = jnp.dot(mixed(3), wv_ref[...], preferred_element_type=F32)
    za = _mm(jnp.dot(mixed(4), a1_ref[...], preferred_element_type=F32),
             a2_ref[...]) + a0_ref[...]
    g_out[...] = _mm(jax.nn.sigmoid(jnp.dot(mixed(5), g1_ref[...], preferred_element_type=F32)),
                     g2_ref[...])
    r_out[...] = r
    v_out[...] = v

    ones_bd = ones_ref[...]
    kk = k * kscale_ref[...]
    kk = kk / jnp.maximum(jnp.sqrt(_head_sum(kk * kk, ones_bd)), 1e-12)
    kk_out[...] = kk

    k_a = ka_ref[...]
    kd_sum = jnp.zeros_like(k)
    for direction, (b_out, kd_out, lw_out) in enumerate(
            ((b0_out, kd0_out, lw0_out), (b1_out, kd1_out, lw1_out))):
        sl = slice(direction * d, (direction + 1) * d)
        lw_out[...] = (-math.exp(-0.5)) * jax.nn.sigmoid(zw[:, sl])
        a_gate = jax.nn.sigmoid(za[:, sl])
        kd = k * (1.0 + (a_gate - 1.0) * k_a)
        kd_out[...] = kd
        b_out[...] = kk * a_gate
        kd_sum = kd_sum + kd
    bonus_out[...] = _head_sum(r * kd_sum * rk_ref[...], ones_bd) * v


def _rwkv_pre(x2d, seq_len, norm, mu, wr, wk, wv, g1, g2, w1, w2, w0, a1, a2, a0,
              k_scale, k_a, r_k, ones_bd, *, tm):
    m, d = x2d.shape
    n_halo = m // SUBLANES
    halo = tm // SUBLANES
    row_spec = pl.BlockSpec((tm, d), lambda i: (i, 0))

    def full(a):
        return pl.BlockSpec(a.shape, lambda i: (0,) * a.ndim)

    params = (norm, mu, wr, wk, wv, g1, g2, w1, w2, w0, a1, a2, a0, k_scale, k_a, r_k, ones_bd)
    return pl.pallas_call(
        functools.partial(_rwkv_pre_kernel, tm=tm, seq_len=seq_len),
        grid=(m // tm,),
        in_specs=[row_spec,
                  pl.BlockSpec((SUBLANES, d), lambda i: (jnp.maximum(i * halo - 1, 0), 0)),
                  pl.BlockSpec((SUBLANES, d), lambda i: (jnp.minimum((i + 1) * halo, n_halo - 1), 0)),
                  ] + [full(p) for p in params],
        out_specs=[row_spec] * 11,
        out_shape=[jax.ShapeDtypeStruct((m, d), F32)] * 11,
        compiler_params=pltpu.CompilerParams(
            dimension_semantics=("arbitrary",), vmem_limit_bytes=VMEM_LIMIT_BYTES),
        name="rwkv_pre",
    )(x2d, x2d, x2d, *params)


def _wkv_kernel(r_ref, v_ref, kk_ref, b_ref, kd_ref, lw_ref, o_ref, state_ref, *, reverse):
    @pl.when(pl.program_id(1) == 0)
    def _():
        state_ref[...] = jnp.zeros_like(state_ref)

    n = SCAN_CHUNK
    d = r_ref.shape[1]
    ti = lax.broadcasted_iota(jnp.int32, (n, n), 0)
    tj = lax.broadcasted_iota(jnp.int32, (n, n), 1)
    cum_mat = ((ti <= tj) if reverse else (ti >= tj)).astype(BF16)
    pi = lax.broadcasted_iota(jnp.int32, (PAIR, PAIR), 0)
    pj = lax.broadcasted_iota(jnp.int32, (PAIR, PAIR), 1)
    strict = (pi < pj) if reverse else (pi > pj)
    incl = (pi <= pj) if reverse else (pi >= pj)
    head0 = lax.broadcasted_iota(jnp.int32, (n, LANES), 1) < HEAD_DIM

    lw = lw_ref[...]
    hi, mid, lo = _split3(lw)
    cum = (jnp.dot(cum_mat, hi, preferred_element_type=F32)
           + jnp.dot(cum_mat, mid, preferred_element_type=F32)
           + jnp.dot(cum_mat, lo, preferred_element_type=F32))
    e_incl = jnp.exp(cum)
    e_excl = jnp.exp(cum - lw)
    e_inv = jnp.exp(-cum)
    a_t = -(kk_ref[...] * e_excl)
    r_t = r_ref[...] * e_incl
    b_t = b_ref[...] * e_inv
    k_t = kd_ref[...] * e_inv
    last = 0 if reverse else n - 1
    chunk_decay = e_incl[last:last + 1, :]

    def expand(x):
        return jnp.concatenate([jnp.where(head0, x, 0.0), jnp.where(head0, 0.0, x)], axis=0)

    tiles = range(d // LANES)
    lanes = [slice(p * LANES, (p + 1) * LANES) for p in tiles]
    ar = [jnp.concatenate([expand(a_t[:, sl]), expand(r_t[:, sl])], axis=0).astype(BF16)
          for sl in lanes]
    bk = [jnp.concatenate([expand(b_t[:, sl]), expand(k_t[:, sl])], axis=0).astype(BF16)
          for sl in lanes]
    v2 = [expand(v_ref[:, sl]).astype(BF16) for sl in lanes]
    state = [state_ref[p] for p in tiles]

    gram = [_mm_nt(ar[p], bk[p]) for p in tiles]
    from_state = [_mm_nt(ar[p], state[p]) for p in tiles]
    power = [jnp.where(strict, g[:PAIR, :PAIR], 0.0).astype(BF16) for g in gram]
    a_ak = [jnp.where(strict, g[:PAIR, PAIR:], 0.0) for g in gram]
    a_r = [jnp.where(jnp.concatenate([incl, incl], axis=1), g[PAIR:, :], 0.0).astype(BF16)
           for g in gram]

    u = [from_state[p][:PAIR] + _mm(a_ak[p], v2[p]) for p in tiles]
    steps = int(math.log2(n))
    for s in range(steps - 1):
        both = [jnp.dot(power[p], jnp.concatenate([power[p], u[p].astype(BF16)], axis=1),
                        preferred_element_type=F32) for p in tiles]
        power = [x[:, :PAIR].astype(BF16) for x in both]
        u = [u[p] + both[p][:, PAIR:] for p in tiles]
    u = [u[p] + _mm(power[p], u[p]) for p in tiles]

    uv = [jnp.concatenate([u[p].astype(BF16), v2[p]], axis=0) for p in tiles]
    out2 = [from_state[p][PAIR:] + _mm(a_r[p], uv[p]) for p in tiles]
    new_state = [_mm_tn(uv[p], bk[p]) for p in tiles]
    for p in tiles:
        o_ref[:, lanes[p]] = out2[p][:n] + out2[p][n:]
        state_ref[p] = (state[p] + new_state[p]) * chunk_decay[:, lanes[p]]


def _wkv_scan(r, v, kk, b, kd, lw, batch, seq_len, *, reverse):
    m, d = r.shape
    n_chunks = seq_len // SCAN_CHUNK

    def idx(bi, c):
        c = (n_chunks - 1 - c) if reverse else c
        return (bi * n_chunks + c, 0)

    spec = pl.BlockSpec((SCAN_CHUNK, d), idx)
    return pl.pallas_call(
        functools.partial(_wkv_kernel, reverse=reverse),
        grid=(batch, n_chunks),
        in_specs=[spec] * 6,
        out_specs=spec,
        out_shape=jax.ShapeDtypeStruct((m, d), F32),
        scratch_shapes=[pltpu.VMEM((d // LANES, PAIR, LANES), F32)],
        compiler_params=pltpu.CompilerParams(
            dimension_semantics=("arbitrary", "arbitrary"), vmem_limit_bytes=VMEM_LIMIT_BYTES),
        name="wkv_rev" if reverse else "wkv_fwd",
    )(r, v, kk, b, kd, lw)


def _rwkv_post_kernel(x_ref, s0_ref, s1_ref, bonus_ref, g_ref, lng_ref, lnb_ref, ones_ref,
                      wo_ref, o_ref):
    ones_bd = ones_ref[...]
    s = s0_ref[...] + s1_ref[...]
    oc = s - _head_sum(s, ones_bd) * (1.0 / HEAD_DIM)
    var = _head_sum(oc * oc, ones_bd) * (1.0 / HEAD_DIM)
    o = oc * lax.rsqrt(var + GN_EPS) * lng_ref[...] + lnb_ref[...] + bonus_ref[...]
    o_ref[...] = x_ref[...] + jnp.dot((o * g_ref[...]).astype(BF16), wo_ref[...],
                                      preferred_element_type=F32)


def _rwkv_post(x2d, s0, s1, bonus, g, ln_g, ln_b, ones_bd, wo, *, tm):
    m, d = x2d.shape
    row_spec = pl.BlockSpec((tm, d), lambda i: (i, 0))

    def full(a):
        return pl.BlockSpec(a.shape, lambda i: (0,) * a.ndim)

    return pl.pallas_call(
        _rwkv_post_kernel,
        grid=(m // tm,),
        in_specs=[row_spec] * 5 + [full(ln_g), full(ln_b), full(ones_bd), full(wo)],
        out_specs=row_spec,
        out_shape=jax.ShapeDtypeStruct((m, d), F32),
        compiler_params=pltpu.CompilerParams(
            dimension_semantics=("arbitrary",), vmem_limit_bytes=VMEM_LIMIT_BYTES),
        name="rwkv_post",
    )(x2d, s0, s1, bonus, g, ln_g, ln_b, ones_bd, wo)


def _ffn_kernel(x_ref, norm_ref, w1_ref, w2_ref, fnorm_ref, o_ref, xn_ref, acc_ref, *, final_norm):
    j = pl.program_id(1)

    @pl.when(j == 0)
    def _():
        xn_ref[...] = _rms(x_ref[...], norm_ref[...]).astype(BF16)
        acc_ref[...] = jnp.zeros_like(acc_ref)

    h = jnp.maximum(jnp.dot(xn_ref[...], w1_ref[...], preferred_element_type=F32), 0.0)
    acc_ref[...] += jnp.dot((h * h).astype(BF16), w2_ref[...], preferred_element_type=F32)

    @pl.when(j == pl.num_programs(1) - 1)
    def _():
        y = x_ref[...] + acc_ref[...]
        o_ref[...] = _rms(y, fnorm_ref[...]) if final_norm else y


def _ffn(x2d, norm, w1, w2, fnorm, *, tm, th, final_norm):
    m, d = x2d.shape
    hidden = w1.shape[1]
    return pl.pallas_call(
        functools.partial(_ffn_kernel, final_norm=final_norm),
        grid=(m // tm, hidden // th),
        in_specs=[pl.BlockSpec((tm, d), lambda i, j: (i, 0)),
                  pl.BlockSpec((1, d), lambda i, j: (0, 0)),
                  pl.BlockSpec((d, th), lambda i, j: (0, j)),
                  pl.BlockSpec((th, d), lambda i, j: (j, 0)),
                  pl.BlockSpec((1, d), lambda i, j: (0, 0))],
        out_specs=pl.BlockSpec((tm, d), lambda i, j: (i, 0)),
        out_shape=jax.ShapeDtypeStruct((m, d), F32),
        scratch_shapes=[pltpu.VMEM((tm, d), BF16), pltpu.VMEM((tm, d), F32)],
        compiler_params=pltpu.CompilerParams(
            dimension_semantics=("arbitrary", "arbitrary"), vmem_limit_bytes=VMEM_LIMIT_BYTES),
        name="ffn_final" if final_norm else "ffn",
    )(x2d, norm, w1, w2, fnorm)


def _gmlp_kernel(x_ref, norm_ref, win_ref, lng_ref, lnb_ref, ws_ref, bs_ref, wout_ref,
                 o_ref, gated_ref, *, tm):
    x = x_ref[...]
    xn = _rms(x, norm_ref[...]).astype(BF16)
    h = jnp.dot(xn, win_ref[...], preferred_element_type=F32)
    h = 0.5 * h * (1.0 + lax.erf(h * (1.0 / math.sqrt(2.0))))
    width = h.shape[1] // 2
    u = h[:, :width]
    v = h[:, width:]
    mean = jnp.mean(v, axis=-1, keepdims=True)
    vc = v - mean
    var = jnp.mean(vc * vc, axis=-1, keepdims=True)
    v = (vc * lax.rsqrt(var + NORM_EPS) * lng_ref[...] + lnb_ref[...]).astype(BF16)
    for c in range(tm // GMLP_CHUNK):
        rows = slice(c * GMLP_CHUNK, (c + 1) * GMLP_CHUNK)
        for grp in range(width // GMLP_GROUP):
            cols = slice(grp * GMLP_GROUP, (grp + 1) * GMLP_GROUP)
            mixed = jnp.dot(ws_ref[grp], v[rows, cols], preferred_element_type=F32) + bs_ref[:, cols]
            gated_ref[rows, cols] = (u[rows, cols] * mixed).astype(BF16)
    o_ref[...] = x + jnp.dot(gated_ref[...], wout_ref[...], preferred_element_type=F32)


def _gmlp(x2d, norm, w_in, ln_g, ln_b, w_s, b_s_full, w_out, *, tm):
    m, d = x2d.shape
    width = w_out.shape[0]
    row_spec = pl.BlockSpec((tm, d), lambda i: (i, 0))

    def full(a):
        return pl.BlockSpec(a.shape, lambda i: (0,) * a.ndim)

    params = (norm, w_in, ln_g, ln_b, w_s, b_s_full, w_out)
    return pl.pallas_call(
        functools.partial(_gmlp_kernel, tm=tm),
        grid=(m // tm,),
        in_specs=[row_spec] + [full(p) for p in params],
        out_specs=row_spec,
        out_shape=jax.ShapeDtypeStruct((m, d), F32),
        scratch_shapes=[pltpu.VMEM((tm, width), BF16)],
        compiler_params=pltpu.CompilerParams(
            dimension_semantics=("arbitrary",), vmem_limit_bytes=VMEM_LIMIT_BYTES),
        name="gmlp",
    )(x2d, *params)


def _block_diag2(w):
    z = jnp.zeros_like(w[0])
    return jnp.concatenate([jnp.concatenate([w[0], z], axis=1),
                            jnp.concatenate([z, w[1]], axis=1)], axis=0)


def kernel(x, mix_norm, ffn_norm, final_norm, rwkv_mu, rwkv_wr, rwkv_wk, rwkv_wv, rwkv_wo, rwkv_w0, rwkv_w1, rwkv_w2, rwkv_a0, rwkv_a1, rwkv_a2, rwkv_g1, rwkv_g2, rwkv_k_k, rwkv_k_a, rwkv_r_k, rwkv_ln_g, rwkv_ln_b, gmlp_w_in, gmlp_ln_g, gmlp_ln_b, gmlp_w_s, gmlp_b_s, gmlp_w_out, ffn_w1, ffn_w2):
    batch, seq_len, d = x.shape
    depth = mix_norm.shape[0]
    m = batch * seq_len
    assert d % LANES == 0 and seq_len % SCAN_CHUNK == 0 and seq_len % GMLP_CHUNK == 0
    tm_pre = min(256, seq_len)
    tm_post = min(512, seq_len)
    tm_ffn = min(1024, m)
    tm_gmlp = min(256, seq_len)
    th_ffn = 1024

    lane = jnp.arange(LANES) // HEAD_DIM
    ones_bd = (lane[:, None] == lane[None, :]).astype(BF16)
    row = lambda a: a.reshape(1, -1).astype(F32)
    cat_dirs = lambda w: jnp.concatenate([w[0], w[1]], axis=-1)

    h = x.reshape(m, d)
    for layer in range(depth):
        j = layer // 2
        norm = row(mix_norm[layer])
        if layer % 2 == 0:
            outs = _rwkv_pre(
                h, seq_len, norm, rwkv_mu[j],
                rwkv_wr[j].astype(BF16), rwkv_wk[j].astype(BF16), rwkv_wv[j].astype(BF16),
                rwkv_g1[j].astype(BF16), rwkv_g2[j].astype(BF16),
                cat_dirs(rwkv_w1[j]).astype(BF16), _block_diag2(rwkv_w2[j]).astype(BF16),
                row(cat_dirs(rwkv_w0[j])),
                cat_dirs(rwkv_a1[j]).astype(BF16), _block_diag2(rwkv_a2[j]).astype(BF16),
                row(cat_dirs(rwkv_a0[j])),
                row(rwkv_k_k[j]), row(rwkv_k_a[j]), row(rwkv_r_k[j]), ones_bd, tm=tm_pre)
            r, v, kk, g, bonus, b0, kd0, lw0, b1, kd1, lw1 = outs
            s0 = _wkv_scan(r, v, kk, b0, kd0, lw0, batch, seq_len, reverse=False)
            s1 = _wkv_scan(r, v, kk, b1, kd1, lw1, batch, seq_len, reverse=True)
            h = _rwkv_post(h, s0, s1, bonus, g, row(rwkv_ln_g[j]), row(rwkv_ln_b[j]), ones_bd,
                           rwkv_wo[j].astype(BF16), tm=tm_post)
        else:
            b_full = jnp.repeat(jnp.transpose(gmlp_b_s[j]), GMLP_GROUP, axis=1)
            h = _gmlp(h, norm, gmlp_w_in[j].astype(BF16), row(gmlp_ln_g[j]), row(gmlp_ln_b[j]),
                      gmlp_w_s[j].astype(BF16), b_full, gmlp_w_out[j].astype(BF16), tm=tm_gmlp)
        last = layer == depth - 1
        h = _ffn(h, row(ffn_norm[layer]), ffn_w1[layer].astype(BF16), ffn_w2[layer].astype(BF16),
                 row(final_norm), tm=tm_ffn, th=th_ffn, final_norm=last)
    return h.reshape(batch, seq_len, d)
```

```python
import functools
import math

import jax
import jax.numpy as jnp
from jax import lax
from jax.experimental import pallas as pl
from jax.experimental.pallas import tpu as pltpu

F32 = jnp.float32
BF16 = jnp.bfloat16

HEAD_DIM = 64
GN_EPS = 64e-5
NORM_EPS = 1e-5
GMLP_CHUNK = 128
GMLP_GROUP = 128

LANES = 128
SUBLANES = 8
MXU_TILE = 256
HEADS_PER_TILE = LANES // HEAD_DIM
VMEM_LIMIT_BYTES = 56 * 1024 * 1024

SCAN_CHUNK = 64
PAIR = HEADS_PER_TILE * SCAN_CHUNK


def _mm(a, b):
    return jnp.dot(a.astype(BF16), b.astype(BF16), preferred_element_type=F32)


def _mm_nt(a, b):
    return lax.dot_general(a.astype(BF16), b.astype(BF16), (((1,), (1,)), ((), ())),
                           preferred_element_type=F32)


def _mm_tn(a, b):
    return lax.dot_general(a.astype(BF16), b.astype(BF16), (((0,), (0,)), ((), ())),
                           preferred_element_type=F32)


def _split3(x):
    hi = x.astype(BF16)
    rem = x - hi.astype(F32)
    mid = rem.astype(BF16)
    lo = (rem - mid.astype(F32)).astype(BF16)
    return hi, mid, lo


def _rms(x, g):
    return x * lax.rsqrt(jnp.mean(x * x, axis=-1, keepdims=True) + NORM_EPS) * g


def _head_sum(x, ones_bd):
    width = ones_bd.shape[0]
    hi = x.astype(BF16)
    lo = (x - hi.astype(F32)).astype(BF16)
    cols = []
    for j in range(x.shape[1] // width):
        sl = slice(j * width, (j + 1) * width)
        cols.append(jnp.dot(hi[:, sl], ones_bd, preferred_element_type=F32)
                    + jnp.dot(lo[:, sl], ones_bd, preferred_element_type=F32))
    return jnp.concatenate(cols, axis=1)


def _rwkv_pre_kernel(x_ref, xprev_ref, xnext_ref, norm_ref, mu_ref,
                     wr_ref, wk_ref, wv_ref, g1_ref, g2_ref,
                     w1_ref, w2_ref, w0_ref, a1_ref, a2_ref, a0_ref,
                     kscale_ref, ka_ref, rk_ref, ones_ref,
                     r_out, v_out, kk_out, g_out, bonus_out,
                     b0_out, kd0_out, lw0_out, b1_out, kd1_out, lw1_out,
                     *, tm, seq_len):
    i = pl.program_id(0)
    d = x_ref.shape[1]
    gain = norm_ref[...]
    xn = _rms(x_ref[...], gain)
    at_start = (i * tm) % seq_len == 0
    at_end = ((i + 1) * tm) % seq_len == 0
    edge_prev = jnp.where(at_start, 0.0, _rms(xprev_ref[SUBLANES - 1:SUBLANES, :], gain))
    edge_next = jnp.where(at_end, 0.0, _rms(xnext_ref[0:1, :], gain))
    row = lax.broadcasted_iota(jnp.int32, (tm, 1), 0)
    x_prev = jnp.where(row == 0, edge_prev, pltpu.roll(xn, 1, axis=0))
    x_next = jnp.where(row == tm - 1, edge_next, pltpu.roll(xn, tm - 1, axis=0))
    xx = 0.5 * (x_prev + x_next) - xn

    def mixed(j):
        return (xn + xx * mu_ref[j:j + 1, :]).astype(BF16)

    r = jnp.dot(mixed(0), wr_ref[...], preferred_element_type=F32)
    zw = _mm(jnp.tanh(jnp.dot(mixed(1), w1_ref[...], preferred_element_type=F32)),
             w2_ref[...]) + w0_ref[...]
    k = jnp.dot(mixed(2), wk_ref[...], preferred_element_type=F32)
    v = jnp.dot(mixed(3), wv_ref[...], preferred_element_type=F32)
    za = _mm(jnp.dot(mixed(4), a1_ref[...], preferred_element_type=F32),
             a2_ref[...]) + a0_ref[...]
    g_out[...] = _mm(jax.nn.sigmoid(jnp.dot(mixed(5), g1_ref[...], preferred_element_type=F32)),
                     g2_ref[...])
    r_out[...] = r
    v_out[...] = v

    ones_bd = ones_ref[...]
    kk = k * kscale_ref[...]
    kk = kk / jnp.maximum(jnp.sqrt(_head_sum(kk * kk, ones_bd)), 1e-12)
    kk_out[...] = kk

    k_a = ka_ref[...]
    kd_sum = jnp.zeros_like(k)
    for direction, (b_out, kd_out, lw_out) in enumerate(
            ((b0_out, kd0_out, lw0_out), (b1_out, kd1_out, lw1_out))):
        sl = slice(direction * d, (direction + 1) * d)
        lw_out[...] = (-math.exp(-0.5)) * jax.nn.sigmoid(zw[:, sl])
        a_gate = jax.nn.sigmoid(za[:, sl])
        kd = k * (1.0 + (a_gate - 1.0) * k_a)
        kd_out[...] = kd
        b_out[...] = kk * a_gate
        kd_sum = kd_sum + kd
    bonus_out[...] = _head_sum(r * kd_sum * rk_ref[...], ones_bd) * v


def _rwkv_pre(x2d, seq_len, norm, mu, wr, wk, wv, g1, g2, w1, w2, w0, a1, a2, a0,
              k_scale, k_a, r_k, ones_bd, *, tm):
    m, d = x2d.shape
    n_halo = m // SUBLANES
    halo = tm // SUBLANES
    row_spec = pl.BlockSpec((tm, d), lambda i: (i, 0))

    def full(a):
        return pl.BlockSpec(a.shape, lambda i: (0,) * a.ndim)

    params = (norm, mu, wr, wk, wv, g1, g2, w1, w2, w0, a1, a2, a0, k_scale, k_a, r_k, ones_bd)
    return pl.pallas_call(
        functools.partial(_rwkv_pre_kernel, tm=tm, seq_len=seq_len),
        grid=(m // tm,),
        in_specs=[row_spec,
                  pl.BlockSpec((SUBLANES, d), lambda i: (jnp.maximum(i * halo - 1, 0), 0)),
                  pl.BlockSpec((SUBLANES, d), lambda i: (jnp.minimum((i + 1) * halo, n_halo - 1), 0)),
                  ] + [full(p) for p in params],
        out_specs=[row_spec] * 11,
        out_shape=[jax.ShapeDtypeStruct((m, d), F32)] * 11,
        compiler_params=pltpu.CompilerParams(
            dimension_semantics=("arbitrary",), vmem_limit_bytes=VMEM_LIMIT_BYTES),
        name="rwkv_pre",
    )(x2d, x2d, x2d, *params)


def _scan_operands(r_ref, v_ref, kk_ref, b_ref, kd_ref, lw_ref, reverse):
    n = SCAN_CHUNK
    ti = lax.broadcasted_iota(jnp.int32, (n, n), 0)
    tj = lax.broadcasted_iota(jnp.int32, (n, n), 1)
    cum_mat = ((ti <= tj) if reverse else (ti >= tj)).astype(BF16)
    lw = lw_ref[...]
    hi, mid, lo = _split3(lw)
    cum = (jnp.dot(cum_mat, hi, preferred_element_type=F32)
           + jnp.dot(cum_mat, mid, preferred_element_type=F32)
           + jnp.dot(cum_mat, lo, preferred_element_type=F32))
    e_incl = jnp.exp(cum)
    e_inv = jnp.exp(-cum)
    last = 0 if reverse else n - 1
    return dict(
        a=(-(kk_ref[...] * jnp.exp(cum - lw))).astype(BF16),
        r=(r_ref[...] * e_incl).astype(BF16),
        b=(b_ref[...] * e_inv).astype(BF16),
        k=(kd_ref[...] * e_inv).astype(BF16),
        v=v_ref[...].astype(BF16),
        chunk_decay=e_incl[last:last + 1, :])


def _wkv_kernel(rf_ref, vf_ref, kkf_ref, bf_ref, kdf_ref, lwf_ref,
                rb_ref, vb_ref, kkb_ref, bb_ref, kdb_ref, lwb_ref,
                of_ref, ob_ref, state_ref):
    @pl.when(pl.program_id(1) == 0)
    def _():
        state_ref[...] = jnp.zeros_like(state_ref)

    n = SCAN_CHUNK
    d = rf_ref.shape[1]
    ci = lax.broadcasted_iota(jnp.int32, (n, LANES), 0)
    cj = lax.broadcasted_iota(jnp.int32, (n, LANES), 1)
    head0 = cj < HEAD_DIM
    cj = cj % HEAD_DIM
    same_head = (lax.broadcasted_iota(jnp.int32, (PAIR, LANES), 0) // HEAD_DIM
                 == lax.broadcasted_iota(jnp.int32, (PAIR, LANES), 1) // HEAD_DIM)

    def expand(x):
        zero = jnp.zeros_like(x)
        return jnp.concatenate([jnp.where(head0, x, zero), jnp.where(head0, zero, x)], axis=0)

    ops = (_scan_operands(rf_ref, vf_ref, kkf_ref, bf_ref, kdf_ref, lwf_ref, False),
           _scan_operands(rb_ref, vb_ref, kkb_ref, bb_ref, kdb_ref, lwb_ref, True))
    chains = [(direction, p) for direction in range(2) for p in range(d // LANES)]
    lanes = [slice(p * LANES, (p + 1) * LANES) for _, p in chains]
    strict = [(ci < cj) if direction else (ci > cj) for direction, _ in chains]
    incl = [(ci <= cj) if direction else (ci >= cj) for direction, _ in chains]
    idx = range(len(chains))

    def tile(name, i):
        return ops[chains[i][0]][name][:, lanes[i]]

    ar = [jnp.concatenate([tile("a", i), tile("r", i)], axis=0) for i in idx]
    bk = [jnp.concatenate([tile("b", i), tile("k", i)], axis=0) for i in idx]
    bk_x = [jnp.concatenate([expand(tile("b", i)), expand(tile("k", i))], axis=0) for i in idx]
    v_x = [expand(tile("v", i)) for i in idx]
    state = [state_ref[direction, p] for direction, p in chains]

    gram = [_mm_nt(ar[i], bk_x[i]) for i in idx]
    from_state = [_mm_nt(ar[i], state[i]) for i in idx]
    power = [jnp.where(strict[i], gram[i][:n, :PAIR], 0.0).astype(BF16) for i in idx]
    a_ak = [jnp.where(strict[i], gram[i][:n, PAIR:], 0.0) for i in idx]
    a_r = [jnp.where(jnp.concatenate([incl[i], incl[i]], axis=1), gram[i][n:, :], 0.0).astype(BF16)
           for i in idx]

    u = [from_state[i][:n] + _mm(a_ak[i], v_x[i]) for i in idx]
    steps = int(math.log2(n))
    for s in range(steps - 1):
        both = [jnp.dot(power[i],
                        jnp.concatenate([expand(power[i]), expand(u[i].astype(BF16))], axis=1),
                        preferred_element_type=F32) for i in idx]
        power = [x[:, :PAIR].astype(BF16) for x in both]
        u = [u[i] + both[i][:, PAIR:] for i in idx]
    u = [u[i] + jnp.dot(power[i], expand(u[i].astype(BF16)), preferred_element_type=F32)
         for i in idx]

    u16 = [x.astype(BF16) for x in u]
    out = [from_state[i][n:] + jnp.dot(a_r[i], jnp.concatenate([expand(u16[i]), v_x[i]], axis=0),
                                       preferred_element_type=F32) for i in idx]
    grown = [_mm_tn(jnp.concatenate([u16[i], tile("v", i)], axis=0), bk[i]) for i in idx]
    for i, (direction, p) in enumerate(chains):
        (ob_ref if direction else of_ref)[:, lanes[i]] = out[i]
        decay = ops[direction]["chunk_decay"][:, lanes[i]]
        state_ref[direction, p] = (state[i] + jnp.where(same_head, grown[i], 0.0)) * decay


def _wkv_scan(r, v, kk, b0, kd0, lw0, b1, kd1, lw1, batch, seq_len):
    m, d = r.shape
    n_chunks = seq_len // SCAN_CHUNK
    fwd = pl.BlockSpec((SCAN_CHUNK, d), lambda bi, c: (bi * n_chunks + c, 0))
    bwd = pl.BlockSpec((SCAN_CHUNK, d), lambda bi, c: (bi * n_chunks + n_chunks - 1 - c, 0))
    return pl.pallas_call(
        _wkv_kernel,
        grid=(batch, n_chunks),
        in_specs=[fwd] * 6 + [bwd] * 6,
        out_specs=[fwd, bwd],
        out_shape=[jax.ShapeDtypeStruct((m, d), F32)] * 2,
        scratch_shapes=[pltpu.VMEM((2, d // LANES, PAIR, LANES), F32)],
        compiler_params=pltpu.CompilerParams(
            dimension_semantics=("arbitrary", "arbitrary"), vmem_limit_bytes=VMEM_LIMIT_BYTES),
        name="wkv_scan",
    )(r, v, kk, b0, kd0, lw0, r, v, kk, b1, kd1, lw1)


def _rwkv_post_kernel(x_ref, s0_ref, s1_ref, bonus_ref, g_ref, lng_ref, lnb_ref, ones_ref,
                      wo_ref, o_ref):
    ones_bd = ones_ref[...]
    s = s0_ref[...] + s1_ref[...]
    oc = s - _head_sum(s, ones_bd) * (1.0 / HEAD_DIM)
    var = _head_sum(oc * oc, ones_bd) * (1.0 / HEAD_DIM)
    o = oc * lax.rsqrt(var + GN_EPS) * lng_ref[...] + lnb_ref[...] + bonus_ref[...]
    o_ref[...] = x_ref[...] + jnp.dot((o * g_ref[...]).astype(BF16), wo_ref[...],
                                      preferred_element_type=F32)


def _rwkv_post(x2d, s0, s1, bonus, g, ln_g, ln_b, ones_bd, wo, *, tm):
    m, d = x2d.shape
    row_spec = pl.BlockSpec((tm, d), lambda i: (i, 0))

    def full(a):
        return pl.BlockSpec(a.shape, lambda i: (0,) * a.ndim)

    return pl.pallas_call(
        _rwkv_post_kernel,
        grid=(m // tm,),
        in_specs=[row_spec] * 5 + [full(ln_g), full(ln_b), full(ones_bd), full(wo)],
        out_specs=row_spec,
        out_shape=jax.ShapeDtypeStruct((m, d), F32),
        compiler_params=pltpu.CompilerParams(
            dimension_semantics=("arbitrary",), vmem_limit_bytes=VMEM_LIMIT_BYTES),
        name="rwkv_post",
    )(x2d, s0, s1, bonus, g, ln_g, ln_b, ones_bd, wo)


def _ffn_kernel(x_ref, norm_ref, w1_ref, w2_ref, fnorm_ref, o_ref, xn_ref, acc_ref, *, final_norm):
    j = pl.program_id(1)

    @pl.when(j == 0)
    def _():
        xn_ref[...] = _rms(x_ref[...], norm_ref[...]).astype(BF16)
        acc_ref[...] = jnp.zeros_like(acc_ref)

    h = jnp.maximum(jnp.dot(xn_ref[...], w1_ref[...], preferred_element_type=F32), 0.0)
    acc_ref[...] += jnp.dot((h * h).astype(BF16), w2_ref[...], preferred_element_type=F32)

    @pl.when(j == pl.num_programs(1) - 1)
    def _():
        y = x_ref[...] + acc_ref[...]
        o_ref[...] = _rms(y, fnorm_ref[...]) if final_norm else y


def _ffn(x2d, norm, w1, w2, fnorm, *, tm, th, final_norm):
    m, d = x2d.shape
    hidden = w1.shape[1]
    return pl.pallas_call(
        functools.partial(_ffn_kernel, final_norm=final_norm),
        grid=(m // tm, hidden // th),
        in_specs=[pl.BlockSpec((tm, d), lambda i, j: (i, 0)),
                  pl.BlockSpec((1, d), lambda i, j: (0, 0)),
                  pl.BlockSpec((d, th), lambda i, j: (0, j)),
                  pl.BlockSpec((th, d), lambda i, j: (j, 0)),
                  pl.BlockSpec((1, d), lambda i, j: (0, 0))],
        out_specs=pl.BlockSpec((tm, d), lambda i, j: (i, 0)),
        out_shape=jax.ShapeDtypeStruct((m, d), F32),
        scratch_shapes=[pltpu.VMEM((tm, d), BF16), pltpu.VMEM((tm, d), F32)],
        compiler_params=pltpu.CompilerParams(
            dimension_semantics=("arbitrary", "arbitrary"), vmem_limit_bytes=VMEM_LIMIT_BYTES),
        name="ffn_final" if final_norm else "ffn",
    )(x2d, norm, w1, w2, fnorm)


def _gmlp_kernel(x_ref, norm_ref, win_ref, lng_ref, lnb_ref, ws_ref, bs_ref, wout_ref,
                 o_ref, gated_ref, *, tm):
    x = x_ref[...]
    xn = _rms(x, norm_ref[...]).astype(BF16)
    h = jnp.dot(xn, win_ref[...], preferred_element_type=F32)
    h = 0.5 * h * (1.0 + lax.erf(h * (1.0 / math.sqrt(2.0))))
    width = h.shape[1] // 2
    u = h[:, :width]
    v = h[:, width:]
    mean = jnp.mean(v, axis=-1, keepdims=True)
    vc = v - mean
    var = jnp.mean(vc * vc, axis=-1, keepdims=True)
    v = (vc * lax.rsqrt(var + NORM_EPS) * lng_ref[...] + lnb_ref[...]).astype(BF16)
    n_chunks = tm // GMLP_CHUNK
    for grp in range(width // GMLP_GROUP):
        cols = slice(grp * GMLP_GROUP, (grp + 1) * GMLP_GROUP)
        stacked = jnp.concatenate(
            [v[c * GMLP_CHUNK:(c + 1) * GMLP_CHUNK, cols] for c in range(n_chunks)], axis=1)
        mixed = jnp.dot(ws_ref[grp], stacked, preferred_element_type=F32)
        for c in range(n_chunks):
            rows = slice(c * GMLP_CHUNK, (c + 1) * GMLP_CHUNK)
            part = mixed[:, c * GMLP_GROUP:(c + 1) * GMLP_GROUP] + bs_ref[:, cols]
            gated_ref[rows, cols] = (u[rows, cols] * part).astype(BF16)
    o_ref[...] = x + jnp.dot(gated_ref[...], wout_ref[...], preferred_element_type=F32)


def _gmlp(x2d, norm, w_in, ln_g, ln_b, w_s, b_s_full, w_out, *, tm):
    m, d = x2d.shape
    width = w_out.shape[0]
    row_spec = pl.BlockSpec((tm, d), lambda i: (i, 0))

    def full(a):
        return pl.BlockSpec(a.shape, lambda i: (0,) * a.ndim)

    params = (norm, w_in, ln_g, ln_b, w_s, b_s_full, w_out)
    return pl.pallas_call(
        functools.partial(_gmlp_kernel, tm=tm),
        grid=(m // tm,),
        in_specs=[row_spec] + [full(p) for p in params],
        out_specs=row_spec,
        out_shape=jax.ShapeDtypeStruct((m, d), F32),
        scratch_shapes=[pltpu.VMEM((tm, width), BF16)],
        compiler_params=pltpu.CompilerParams(
            dimension_semantics=("arbitrary",), vmem_limit_bytes=VMEM_LIMIT_BYTES),
        name="gmlp",
    )(x2d, *params)


def _block_diag2(w):
    z = jnp.zeros_like(w[0])
    return jnp.concatenate([jnp.concatenate([w[0], z], axis=1),
                            jnp.concatenate([z, w[1]], axis=1)], axis=0)


def kernel(x, mix_norm, ffn_norm, final_norm, rwkv_mu, rwkv_wr, rwkv_wk, rwkv_wv, rwkv_wo, rwkv_w0, rwkv_w1, rwkv_w2, rwkv_a0, rwkv_a1, rwkv_a2, rwkv_g1, rwkv_g2, rwkv_k_k, rwkv_k_a, rwkv_r_k, rwkv_ln_g, rwkv_ln_b, gmlp_w_in, gmlp_ln_g, gmlp_ln_b, gmlp_w_s, gmlp_b_s, gmlp_w_out, ffn_w1, ffn_w2):
    batch, seq_len, d = x.shape
    depth = mix_norm.shape[0]
    m = batch * seq_len
    assert d % LANES == 0 and seq_len % SCAN_CHUNK == 0 and seq_len % GMLP_CHUNK == 0
    tm_pre = min(256, seq_len)
    tm_post = min(512, seq_len)
    tm_ffn = min(1024, m)
    tm_gmlp = min(256, seq_len)
    th_ffn = 1024

    lane = jnp.arange(MXU_TILE) // HEAD_DIM
    ones_bd = (lane[:, None] == lane[None, :]).astype(BF16)
    row = lambda a: a.reshape(1, -1).astype(F32)
    cat_dirs = lambda w: jnp.concatenate([w[0], w[1]], axis=-1)

    h = x.reshape(m, d)
    for layer in range(depth):
        j = layer // 2
        norm = row(mix_norm[layer])
        if layer % 2 == 0:
            outs = _rwkv_pre(
                h, seq_len, norm, rwkv_mu[j],
                rwkv_wr[j].astype(BF16), rwkv_wk[j].astype(BF16), rwkv_wv[j].astype(BF16),
                rwkv_g1[j].astype(BF16), rwkv_g2[j].astype(BF16),
                cat_dirs(rwkv_w1[j]).astype(BF16), _block_diag2(rwkv_w2[j]).astype(BF16),
                row(cat_dirs(rwkv_w0[j])),
                cat_dirs(rwkv_a1[j]).astype(BF16), _block_diag2(rwkv_a2[j]).astype(BF16),
                row(cat_dirs(rwkv_a0[j])),
                row(rwkv_k_k[j]), row(rwkv_k_a[j]), row(rwkv_r_k[j]), ones_bd, tm=tm_pre)
            r, v, kk, g, bonus, b0, kd0, lw0, b1, kd1, lw1 = outs
            s0, s1 = _wkv_scan(r, v, kk, b0, kd0, lw0, b1, kd1, lw1, batch, seq_len)
            h = _rwkv_post(h, s0, s1, bonus, g, row(rwkv_ln_g[j]), row(rwkv_ln_b[j]), ones_bd,
                           rwkv_wo[j].astype(BF16), tm=tm_post)
        else:
            b_full = jnp.repeat(jnp.transpose(gmlp_b_s[j]), GMLP_GROUP, axis=1)
            h = _gmlp(h, norm, gmlp_w_in[j].astype(BF16), row(gmlp_ln_g[j]), row(gmlp_ln_b[j]),
                      gmlp_w_s[j].astype(BF16), b_full, gmlp_w_out[j].astype(BF16), tm=tm_gmlp)
        last = layer == depth - 1
        h = _ffn(h, row(ffn_norm[layer]), ffn_w1[layer].astype(BF16), ffn_w2[layer].astype(BF16),
                 row(final_norm), tm=tm_ffn, th=th_ffn, final_norm=last)
    return h.reshape(batch, seq_len, d)
```

```python
import functools
import math

import jax
import jax.numpy as jnp
from jax import lax
from jax.experimental import pallas as pl
from jax.experimental.pallas import tpu as pltpu

F32 = jnp.float32
BF16 = jnp.bfloat16

HEAD_DIM = 64
GN_EPS = 64e-5
NORM_EPS = 1e-5
GMLP_CHUNK = 128
GMLP_GROUP = 128

LANES = 128
SUBLANES = 8
MXU_TILE = 256
HEADS_PER_TILE = LANES // HEAD_DIM
VMEM_LIMIT_BYTES = 56 * 1024 * 1024

SCAN_CHUNK = 64
PAIR = HEADS_PER_TILE * SCAN_CHUNK


def _mm(a, b):
    return jnp.dot(a.astype(BF16), b.astype(BF16), preferred_element_type=F32)


def _mm_nt(a, b):
    return lax.dot_general(a.astype(BF16), b.astype(BF16), (((1,), (1,)), ((), ())),
                           preferred_element_type=F32)


def _mm_tn(a, b):
    return lax.dot_general(a.astype(BF16), b.astype(BF16), (((0,), (0,)), ((), ())),
                           preferred_element_type=F32)


def _split3(x):
    hi = x.astype(BF16)
    rem = x - hi.astype(F32)
    mid = rem.astype(BF16)
    lo = (rem - mid.astype(F32)).astype(BF16)
    return hi, mid, lo


def _sigmoid(x):
    return 0.5 * jnp.tanh(0.5 * x) + 0.5


def _rms(x, g):
    return x * lax.rsqrt(jnp.mean(x * x, axis=-1, keepdims=True) + NORM_EPS) * g


def _head_sum(x, ones_bd):
    width = ones_bd.shape[0]
    hi = x.astype(BF16)
    lo = (x - hi.astype(F32)).astype(BF16)
    cols = []
    for j in range(x.shape[1] // width):
        sl = slice(j * width, (j + 1) * width)
        cols.append(jnp.dot(hi[:, sl], ones_bd, preferred_element_type=F32)
                    + jnp.dot(lo[:, sl], ones_bd, preferred_element_type=F32))
    return jnp.concatenate(cols, axis=1)


def _rwkv_pre_kernel(x_ref, xprev_ref, xnext_ref, norm_ref, mu_ref,
                     wr_ref, wk_ref, wv_ref, g1_ref, g2_ref,
                     w1_ref, w2_ref, w0_ref, a1_ref, a2_ref, a0_ref,
                     kscale_ref, ka_ref, rk_ref, ones_ref,
                     r_out, v_out, kk_out, g_out, bonus_out,
                     b0_out, kd0_out, lw0_out, b1_out, kd1_out, lw1_out,
                     *, tm, seq_len):
    i = pl.program_id(0)
    d = x_ref.shape[1]
    gain = norm_ref[...]
    xn = _rms(x_ref[...], gain)
    at_start = (i * tm) % seq_len == 0
    at_end = ((i + 1) * tm) % seq_len == 0
    edge_prev = jnp.where(at_start, 0.0, _rms(xprev_ref[SUBLANES - 1:SUBLANES, :], gain))
    edge_next = jnp.where(at_end, 0.0, _rms(xnext_ref[0:1, :], gain))
    row = lax.broadcasted_iota(jnp.int32, (tm, 1), 0)
    x_prev = jnp.where(row == 0, edge_prev, pltpu.roll(xn, 1, axis=0))
    x_next = jnp.where(row == tm - 1, edge_next, pltpu.roll(xn, tm - 1, axis=0))
    xx = 0.5 * (x_prev + x_next) - xn

    def mixed(j):
        return (xn + xx * mu_ref[j:j + 1, :]).astype(BF16)

    r = jnp.dot(mixed(0), wr_ref[...], preferred_element_type=F32)
    zw = _mm(jnp.tanh(jnp.dot(mixed(1), w1_ref[...], preferred_element_type=F32)),
             w2_ref[...]) + w0_ref[...]
    k = jnp.dot(mixed(2), wk_ref[...], preferred_element_type=F32)
    v = jnp.dot(mixed(3), wv_ref[...], preferred_element_type=F32)
    za = _mm(jnp.dot(mixed(4), a1_ref[...], preferred_element_type=F32),
             a2_ref[...]) + a0_ref[...]
    g_out[...] = _mm(_sigmoid(jnp.dot(mixed(5), g1_ref[...], preferred_element_type=F32)),
                     g2_ref[...])
    r_out[...] = r
    v_out[...] = v

    ones_bd = ones_ref[...]
    kk = k * kscale_ref[...]
    kk = kk * lax.rsqrt(jnp.maximum(_head_sum(kk * kk, ones_bd), 1e-24))
    kk_out[...] = kk

    k_a = ka_ref[...]
    kd_sum = jnp.zeros_like(k)
    for direction, (b_out, kd_out, lw_out) in enumerate(
            ((b0_out, kd0_out, lw0_out), (b1_out, kd1_out, lw1_out))):
        sl = slice(direction * d, (direction + 1) * d)
        lw_out[...] = (-math.exp(-0.5)) * _sigmoid(zw[:, sl])
        a_gate = _sigmoid(za[:, sl])
        kd = k * (1.0 + (a_gate - 1.0) * k_a)
        kd_out[...] = kd
        b_out[...] = kk * a_gate
        kd_sum = kd_sum + kd
    bonus_out[...] = _head_sum(r * kd_sum * rk_ref[...], ones_bd) * v


def _rwkv_pre(x2d, seq_len, norm, mu, wr, wk, wv, g1, g2, w1, w2, w0, a1, a2, a0,
              k_scale, k_a, r_k, ones_bd, *, tm):
    m, d = x2d.shape
    n_halo = m // SUBLANES
    halo = tm // SUBLANES
    row_spec = pl.BlockSpec((tm, d), lambda i: (i, 0))

    def full(a):
        return pl.BlockSpec(a.shape, lambda i: (0,) * a.ndim)

    params = (norm, mu, wr, wk, wv, g1, g2, w1, w2, w0, a1, a2, a0, k_scale, k_a, r_k, ones_bd)
    return pl.pallas_call(
        functools.partial(_rwkv_pre_kernel, tm=tm, seq_len=seq_len),
        grid=(m // tm,),
        in_specs=[row_spec,
                  pl.BlockSpec((SUBLANES, d), lambda i: (jnp.maximum(i * halo - 1, 0), 0)),
                  pl.BlockSpec((SUBLANES, d), lambda i: (jnp.minimum((i + 1) * halo, n_halo - 1), 0)),
                  ] + [full(p) for p in params],
        out_specs=[row_spec] * 11,
        out_shape=[jax.ShapeDtypeStruct((m, d), F32)] * 11,
        compiler_params=pltpu.CompilerParams(
            dimension_semantics=("arbitrary",), vmem_limit_bytes=VMEM_LIMIT_BYTES),
        name="rwkv_pre",
    )(x2d, x2d, x2d, *params)


def _scan_operands(r_ref, v_ref, kk_ref, b_ref, kd_ref, lw_ref, bi, reverse):
    n = SCAN_CHUNK
    ti = lax.broadcasted_iota(jnp.int32, (n, n), 0)
    tj = lax.broadcasted_iota(jnp.int32, (n, n), 1)
    cum_mat = ((ti <= tj) if reverse else (ti >= tj)).astype(BF16)
    lw = lw_ref[bi]
    hi, mid, lo = _split3(lw)
    cum = (jnp.dot(cum_mat, hi, preferred_element_type=F32)
           + jnp.dot(cum_mat, mid, preferred_element_type=F32)
           + jnp.dot(cum_mat, lo, preferred_element_type=F32))
    e_incl = jnp.exp(cum)
    e_inv = jnp.exp(-cum)
    last = 0 if reverse else n - 1
    return dict(
        a=(-(kk_ref[bi] * jnp.exp(cum - lw))).astype(BF16),
        r=(r_ref[bi] * e_incl).astype(BF16),
        b=(b_ref[bi] * e_inv).astype(BF16),
        k=(kd_ref[bi] * e_inv).astype(BF16),
        v=v_ref[bi].astype(BF16),
        chunk_decay=e_incl[last:last + 1, :])


def _wkv_kernel(rf_ref, vf_ref, kkf_ref, bf_ref, kdf_ref, lwf_ref,
                rb_ref, vb_ref, kkb_ref, bb_ref, kdb_ref, lwb_ref,
                of_ref, ob_ref, state_ref):
    @pl.when(pl.program_id(0) == 0)
    def _():
        state_ref[...] = jnp.zeros_like(state_ref)

    n = SCAN_CHUNK
    batch, _, d = rf_ref.shape
    ci = lax.broadcasted_iota(jnp.int32, (n, LANES), 0)
    cj = lax.broadcasted_iota(jnp.int32, (n, LANES), 1)
    head0 = cj < HEAD_DIM
    cj = cj % HEAD_DIM
    same_head = (lax.broadcasted_iota(jnp.int32, (PAIR, LANES), 0) // HEAD_DIM
                 == lax.broadcasted_iota(jnp.int32, (PAIR, LANES), 1) // HEAD_DIM)

    def expand(x):
        zero = jnp.zeros_like(x)
        return jnp.concatenate([jnp.where(head0, x, zero), jnp.where(head0, zero, x)], axis=0)

    ops = [(_scan_operands(rf_ref, vf_ref, kkf_ref, bf_ref, kdf_ref, lwf_ref, bi, False),
            _scan_operands(rb_ref, vb_ref, kkb_ref, bb_ref, kdb_ref, lwb_ref, bi, True))
           for bi in range(batch)]

    def run_stages(chains):
        lanes = [slice(p * LANES, (p + 1) * LANES) for _, _, p in chains]
        strict = [(ci < cj) if rev else (ci > cj) for _, rev, _ in chains]
        incl = [(ci <= cj) if rev else (ci >= cj) for _, rev, _ in chains]
        idx = range(len(chains))

        def tile(name, i):
            bi, direction, _ = chains[i]
            return ops[bi][direction][name][:, lanes[i]]

        ar = [jnp.concatenate([tile("a", i), tile("r", i)], axis=0) for i in idx]
        bk = [jnp.concatenate([tile("b", i), tile("k", i)], axis=0) for i in idx]
        bk_x = [jnp.concatenate([expand(tile("b", i)), expand(tile("k", i))], axis=0) for i in idx]
        v_x = [expand(tile("v", i)) for i in idx]
        state = [state_ref[bi, direction, p] for bi, direction, p in chains]

        gram = [_mm_nt(ar[i], bk_x[i]) for i in idx]
        from_state = [_mm_nt(ar[i], state[i]) for i in idx]
        power = [jnp.where(strict[i], gram[i][:n, :PAIR], 0.0).astype(BF16) for i in idx]
        a_ak = [jnp.where(strict[i], gram[i][:n, PAIR:], 0.0) for i in idx]
        a_r = [jnp.where(jnp.concatenate([incl[i], incl[i]], axis=1), gram[i][n:, :], 0.0).astype(BF16)
               for i in idx]

        u = [from_state[i][:n] + _mm(a_ak[i], v_x[i]) for i in idx]
        steps = int(math.log2(n))
        for s in range(steps - 1):
            both = [jnp.dot(power[i],
                            jnp.concatenate([expand(power[i]), expand(u[i].astype(BF16))], axis=1),
                            preferred_element_type=F32) for i in idx]
            power = [x[:, :PAIR].astype(BF16) for x in both]
            u = [u[i] + both[i][:, PAIR:] for i in idx]
        u = [u[i] + jnp.dot(power[i], expand(u[i].astype(BF16)), preferred_element_type=F32)
             for i in idx]

        u16 = [x.astype(BF16) for x in u]
        out = [from_state[i][n:] + jnp.dot(a_r[i], jnp.concatenate([expand(u16[i]), v_x[i]], axis=0),
                                           preferred_element_type=F32) for i in idx]
        grown = [_mm_tn(jnp.concatenate([u16[i], tile("v", i)], axis=0), bk[i]) for i in idx]
        for i, (bi, direction, p) in enumerate(chains):
            (ob_ref if direction else of_ref)[bi, :, lanes[i]] = out[i]
            decay = ops[bi][direction]["chunk_decay"][:, lanes[i]]
            state_ref[bi, direction, p] = (state[i] + jnp.where(same_head, grown[i], 0.0)) * decay

    run_stages([(bi, direction, p) for bi in range(batch) for direction in range(2)
                for p in range(d // LANES)])


def _wkv_scan(r, v, kk, b0, kd0, lw0, b1, kd1, lw1, batch, seq_len):
    m, d = r.shape
    n_chunks = seq_len // SCAN_CHUNK
    fwd = pl.BlockSpec((batch, SCAN_CHUNK, d), lambda c: (0, c, 0))
    bwd = pl.BlockSpec((batch, SCAN_CHUNK, d), lambda c: (0, n_chunks - 1 - c, 0))
    rows = lambda a: a.reshape(batch, seq_len, d)
    r, v, kk, b0, kd0, lw0, b1, kd1, lw1 = map(rows, (r, v, kk, b0, kd0, lw0, b1, kd1, lw1))
    s0, s1 = pl.pallas_call(
        _wkv_kernel,
        grid=(n_chunks,),
        in_specs=[fwd] * 6 + [bwd] * 6,
        out_specs=[fwd, bwd],
        out_shape=[jax.ShapeDtypeStruct((batch, seq_len, d), F32)] * 2,
        scratch_shapes=[pltpu.VMEM((batch, 2, d // LANES, PAIR, LANES), F32)],
        compiler_params=pltpu.CompilerParams(
            dimension_semantics=("arbitrary",), vmem_limit_bytes=VMEM_LIMIT_BYTES),
        name="wkv_scan",
    )(r, v, kk, b0, kd0, lw0, r, v, kk, b1, kd1, lw1)
    return s0.reshape(m, d), s1.reshape(m, d)


def _rwkv_post_kernel(x_ref, s0_ref, s1_ref, bonus_ref, g_ref, lng_ref, lnb_ref, ones_ref,
                      wo_ref, o_ref):
    ones_bd = ones_ref[...]
    s = s0_ref[...] + s1_ref[...]
    oc = s - _head_sum(s, ones_bd) * (1.0 / HEAD_DIM)
    var = _head_sum(oc * oc, ones_bd) * (1.0 / HEAD_DIM)
    o = oc * lax.rsqrt(var + GN_EPS) * lng_ref[...] + lnb_ref[...] + bonus_ref[...]
    o_ref[...] = x_ref[...] + jnp.dot((o * g_ref[...]).astype(BF16), wo_ref[...],
                                      preferred_element_type=F32)


def _rwkv_post(x2d, s0, s1, bonus, g, ln_g, ln_b, ones_bd, wo, *, tm):
    m, d = x2d.shape
    row_spec = pl.BlockSpec((tm, d), lambda i: (i, 0))

    def full(a):
        return pl.BlockSpec(a.shape, lambda i: (0,) * a.ndim)

    return pl.pallas_call(
        _rwkv_post_kernel,
        grid=(m // tm,),
        in_specs=[row_spec] * 5 + [full(ln_g), full(ln_b), full(ones_bd), full(wo)],
        out_specs=row_spec,
        out_shape=jax.ShapeDtypeStruct((m, d), F32),
        compiler_params=pltpu.CompilerParams(
            dimension_semantics=("arbitrary",), vmem_limit_bytes=VMEM_LIMIT_BYTES),
        name="rwkv_post",
    )(x2d, s0, s1, bonus, g, ln_g, ln_b, ones_bd, wo)


def _ffn_kernel(x_ref, norm_ref, w1_ref, w2_ref, fnorm_ref, o_ref, xn_ref, acc_ref, *, final_norm):
    j = pl.program_id(1)

    @pl.when(j == 0)
    def _():
        xn_ref[...] = _rms(x_ref[...], norm_ref[...]).astype(BF16)
        acc_ref[...] = jnp.zeros_like(acc_ref)

    h = jnp.maximum(jnp.dot(xn_ref[...], w1_ref[...], preferred_element_type=F32), 0.0)
    acc_ref[...] += jnp.dot((h * h).astype(BF16), w2_ref[...], preferred_element_type=F32)

    @pl.when(j == pl.num_programs(1) - 1)
    def _():
        y = x_ref[...] + acc_ref[...]
        o_ref[...] = _rms(y, fnorm_ref[...]) if final_norm else y


def _ffn(x2d, norm, w1, w2, fnorm, *, tm, th, final_norm):
    m, d = x2d.shape
    hidden = w1.shape[1]
    return pl.pallas_call(
        functools.partial(_ffn_kernel, final_norm=final_norm),
        grid=(m // tm, hidden // th),
        in_specs=[pl.BlockSpec((tm, d), lambda i, j: (i, 0)),
                  pl.BlockSpec((1, d), lambda i, j: (0, 0)),
                  pl.BlockSpec((d, th), lambda i, j: (0, j)),
                  pl.BlockSpec((th, d), lambda i, j: (j, 0)),
                  pl.BlockSpec((1, d), lambda i, j: (0, 0))],
        out_specs=pl.BlockSpec((tm, d), lambda i, j: (i, 0)),
        out_shape=jax.ShapeDtypeStruct((m, d), F32),
        scratch_shapes=[pltpu.VMEM((tm, d), BF16), pltpu.VMEM((tm, d), F32)],
        compiler_params=pltpu.CompilerParams(
            dimension_semantics=("arbitrary", "arbitrary"), vmem_limit_bytes=VMEM_LIMIT_BYTES),
        name="ffn_final" if final_norm else "ffn",
    )(x2d, norm, w1, w2, fnorm)


def _gmlp_kernel(x_ref, norm_ref, win_ref, lng_ref, lnb_ref, ws_ref, bs_ref, wout_ref,
                 o_ref, gated_ref, *, tm):
    x = x_ref[...]
    xn = _rms(x, norm_ref[...]).astype(BF16)
    h = jnp.dot(xn, win_ref[...], preferred_element_type=F32)
    h = 0.5 * h * (1.0 + lax.erf(h * (1.0 / math.sqrt(2.0))))
    width = h.shape[1] // 2
    u = h[:, :width]
    v = h[:, width:]
    mean = jnp.mean(v, axis=-1, keepdims=True)
    vc = v - mean
    var = jnp.mean(vc * vc, axis=-1, keepdims=True)
    v = (vc * lax.rsqrt(var + NORM_EPS) * lng_ref[...] + lnb_ref[...]).astype(BF16)
    n_chunks = tm // GMLP_CHUNK
    for grp in range(width // GMLP_GROUP):
        cols = slice(grp * GMLP_GROUP, (grp + 1) * GMLP_GROUP)
        stacked = jnp.concatenate(
            [v[c * GMLP_CHUNK:(c + 1) * GMLP_CHUNK, cols] for c in range(n_chunks)], axis=1)
        mixed = jnp.dot(ws_ref[grp], stacked, preferred_element_type=F32)
        for c in range(n_chunks):
            rows = slice(c * GMLP_CHUNK, (c + 1) * GMLP_CHUNK)
            part = mixed[:, c * GMLP_GROUP:(c + 1) * GMLP_GROUP] + bs_ref[:, cols]
            gated_ref[rows, cols] = (u[rows, cols] * part).astype(BF16)
    o_ref[...] = x + jnp.dot(gated_ref[...], wout_ref[...], preferred_element_type=F32)


def _gmlp(x2d, norm, w_in, ln_g, ln_b, w_s, b_s_full, w_out, *, tm):
    m, d = x2d.shape
    width = w_out.shape[0]
    row_spec = pl.BlockSpec((tm, d), lambda i: (i, 0))

    def full(a):
        return pl.BlockSpec(a.shape, lambda i: (0,) * a.ndim)

    params = (norm, w_in, ln_g, ln_b, w_s, b_s_full, w_out)
    return pl.pallas_call(
        functools.partial(_gmlp_kernel, tm=tm),
        grid=(m // tm,),
        in_specs=[row_spec] + [full(p) for p in params],
        out_specs=row_spec,
        out_shape=jax.ShapeDtypeStruct((m, d), F32),
        scratch_shapes=[pltpu.VMEM((tm, width), BF16)],
        compiler_params=pltpu.CompilerParams(
            dimension_semantics=("arbitrary",), vmem_limit_bytes=VMEM_LIMIT_BYTES),
        name="gmlp",
    )(x2d, *params)


def _block_diag2(w):
    z = jnp.zeros_like(w[0])
    return jnp.concatenate([jnp.concatenate([w[0], z], axis=1),
                            jnp.concatenate([z, w[1]], axis=1)], axis=0)


def kernel(x, mix_norm, ffn_norm, final_norm, rwkv_mu, rwkv_wr, rwkv_wk, rwkv_wv, rwkv_wo, rwkv_w0, rwkv_w1, rwkv_w2, rwkv_a0, rwkv_a1, rwkv_a2, rwkv_g1, rwkv_g2, rwkv_k_k, rwkv_k_a, rwkv_r_k, rwkv_ln_g, rwkv_ln_b, gmlp_w_in, gmlp_ln_g, gmlp_ln_b, gmlp_w_s, gmlp_b_s, gmlp_w_out, ffn_w1, ffn_w2):
    batch, seq_len, d = x.shape
    depth = mix_norm.shape[0]
    m = batch * seq_len
    assert d % LANES == 0 and seq_len % SCAN_CHUNK == 0 and seq_len % GMLP_CHUNK == 0
    tm_pre = min(256, seq_len)
    tm_post = min(512, seq_len)
    tm_ffn = min(1024, m)
    tm_gmlp = min(256, seq_len)
    th_ffn = 1024

    lane = jnp.arange(MXU_TILE) // HEAD_DIM
    ones_bd = (lane[:, None] == lane[None, :]).astype(BF16)
    row = lambda a: a.reshape(1, -1).astype(F32)
    cat_dirs = lambda w: jnp.concatenate([w[0], w[1]], axis=-1)

    h = x.reshape(m, d)
    for layer in range(depth):
        j = layer // 2
        norm = row(mix_norm[layer])
        if layer % 2 == 0:
            outs = _rwkv_pre(
                h, seq_len, norm, rwkv_mu[j],
                rwkv_wr[j].astype(BF16), rwkv_wk[j].astype(BF16), rwkv_wv[j].astype(BF16),
                rwkv_g1[j].astype(BF16), rwkv_g2[j].astype(BF16),
                cat_dirs(rwkv_w1[j]).astype(BF16), _block_diag2(rwkv_w2[j]).astype(BF16),
                row(cat_dirs(rwkv_w0[j])),
                cat_dirs(rwkv_a1[j]).astype(BF16), _block_diag2(rwkv_a2[j]).astype(BF16),
                row(cat_dirs(rwkv_a0[j])),
                row(rwkv_k_k[j]), row(rwkv_k_a[j]), row(rwkv_r_k[j]), ones_bd, tm=tm_pre)
            r, v, kk, g, bonus, b0, kd0, lw0, b1, kd1, lw1 = outs
            s0, s1 = _wkv_scan(r, v, kk, b0, kd0, lw0, b1, kd1, lw1, batch, seq_len)
            h = _rwkv_post(h, s0, s1, bonus, g, row(rwkv_ln_g[j]), row(rwkv_ln_b[j]), ones_bd,
                           rwkv_wo[j].astype(BF16), tm=tm_post)
        else:
            b_full = jnp.repeat(jnp.transpose(gmlp_b_s[j]), GMLP_GROUP, axis=1)
            h = _gmlp(h, norm, gmlp_w_in[j].astype(BF16), row(gmlp_ln_g[j]), row(gmlp_ln_b[j]),
                      gmlp_w_s[j].astype(BF16), b_full, gmlp_w_out[j].astype(BF16), tm=tm_gmlp)
        last = layer == depth - 1
        h = _ffn(h, row(ffn_norm[layer]), ffn_w1[layer].astype(BF16), ffn_w2[layer].astype(BF16),
                 row(final_norm), tm=tm_ffn, th=th_ffn, final_norm=last)
    return h.reshape(batch, seq_len, d)
```

```python
import functools
import math

import jax
import jax.numpy as jnp
from jax import lax
from jax.experimental import pallas as pl
from jax.experimental.pallas import tpu as pltpu

F32 = jnp.float32
BF16 = jnp.bfloat16

HEAD_DIM = 64
GN_EPS = 64e-5
NORM_EPS = 1e-5
GMLP_CHUNK = 128
GMLP_GROUP = 128

LANES = 128
SUBLANES = 8
MXU_TILE = 256
HEADS_PER_TILE = LANES // HEAD_DIM
VMEM_LIMIT_BYTES = 56 * 1024 * 1024

SCAN_CHUNK = 64
PAIR = HEADS_PER_TILE * SCAN_CHUNK


def _mm(a, b):
    return jnp.dot(a.astype(BF16), b.astype(BF16), preferred_element_type=F32)


def _mm_nt(a, b):
    return lax.dot_general(a.astype(BF16), b.astype(BF16), (((1,), (1,)), ((), ())),
                           preferred_element_type=F32)


def _mm_tn(a, b):
    return lax.dot_general(a.astype(BF16), b.astype(BF16), (((0,), (0,)), ((), ())),
                           preferred_element_type=F32)


def _split3(x):
    hi = x.astype(BF16)
    rem = x - hi.astype(F32)
    mid = rem.astype(BF16)
    lo = (rem - mid.astype(F32)).astype(BF16)
    return hi, mid, lo


def _sigmoid_of_twice(half_x):
    return 0.5 * jnp.tanh(half_x) + 0.5


def _rms(x, g):
    return x * lax.rsqrt(jnp.mean(x * x, axis=-1, keepdims=True) + NORM_EPS) * g


def _head_sum(x, ones_bd):
    width = ones_bd.shape[0]
    xb = x.astype(BF16)
    cols = [jnp.dot(xb[:, j * width:(j + 1) * width], ones_bd, preferred_element_type=F32)
            for j in range(x.shape[1] // width)]
    return jnp.concatenate(cols, axis=1)


def _rwkv_pre_kernel(x_ref, xprev_ref, xnext_ref, norm_ref, mu_ref,
                     wr_ref, wk_ref, wv_ref, g1_ref, g2_ref,
                     w1_ref, w2_ref, w0_ref, a1_ref, a2_ref, a0_ref,
                     kscale_ref, ka_ref, rk_ref, ones_ref,
                     r_out, v_out, kk_out, g_out, bonus_out,
                     b0_out, kd0_out, lw0_out, b1_out, kd1_out, lw1_out,
                     *, tm, sub, seq_len):
    i = pl.program_id(0)
    d = x_ref.shape[1]
    gain = norm_ref[...]
    xn = _rms(x_ref[...], gain)
    at_start = (i * tm) % seq_len == 0
    at_end = ((i + 1) * tm) % seq_len == 0
    edge_prev = jnp.where(at_start, 0.0, _rms(xprev_ref[SUBLANES - 1:SUBLANES, :], gain))
    edge_next = jnp.where(at_end, 0.0, _rms(xnext_ref[0:1, :], gain))
    row = lax.broadcasted_iota(jnp.int32, (tm, 1), 0)
    x_prev = jnp.where(row == 0, edge_prev, pltpu.roll(xn, 1, axis=0))
    x_next = jnp.where(row == tm - 1, edge_next, pltpu.roll(xn, tm - 1, axis=0))
    xx = 0.5 * (x_prev + x_next) - xn
    ones_bd = ones_ref[...]
    k_a = ka_ref[...]

    def project(rows):
        def mixed(j):
            return (xn[rows] + xx[rows] * mu_ref[j:j + 1, :]).astype(BF16)

        dot = functools.partial(jnp.dot, preferred_element_type=F32)
        w_mid = dot(mixed(1), w1_ref[...])
        a_mid = dot(mixed(4), a1_ref[...])
        g_mid = dot(mixed(5), g1_ref[...])
        zw = _mm(jnp.tanh(w_mid), w2_ref[...]) + w0_ref[...]
        za = _mm(a_mid, a2_ref[...]) + a0_ref[...]
        g_out[rows, :] = _mm(_sigmoid_of_twice(g_mid), g2_ref[...])
        k = dot(mixed(2), wk_ref[...])
        r = dot(mixed(0), wr_ref[...])
        v = dot(mixed(3), wv_ref[...])
        return zw, za, k, r, v

    def epilogue(rows, zw, za, k, r, v):
        r_out[rows, :] = r
        v_out[rows, :] = v
        kk = k * kscale_ref[...]
        kk = kk * lax.rsqrt(jnp.maximum(_head_sum(kk * kk, ones_bd), 1e-24))
        kk_out[rows, :] = kk
        k_keep = k * (1.0 - k_a)
        k_gated = k * k_a
        kd = []
        for direction, (b_out, kd_out, lw_out) in enumerate(
                ((b0_out, kd0_out, lw0_out), (b1_out, kd1_out, lw1_out))):
            sl = slice(direction * d, (direction + 1) * d)
            half_c = -0.5 * math.exp(-0.5)
            lw_out[rows, :] = half_c * jnp.tanh(zw[:, sl]) + half_c
            a_gate = _sigmoid_of_twice(za[:, sl])
            kd.append(k_keep + k_gated * a_gate)
            kd_out[rows, :] = kd[direction]
            b_out[rows, :] = kk * a_gate
        bonus_out[rows, :] = _head_sum(r * (kd[0] + kd[1]) * rk_ref[...], ones_bd) * v

    sub_rows = [slice(s * sub, (s + 1) * sub) for s in range(tm // sub)]
    projected = [project(rows) for rows in sub_rows]
    for rows, outs in zip(sub_rows, projected):
        epilogue(rows, *outs)


def _rwkv_pre(x2d, seq_len, norm, mu, wr, wk, wv, g1, g2, w1, w2, w0, a1, a2, a0,
              k_scale, k_a, r_k, ones_bd, *, tm, sub):
    m, d = x2d.shape
    n_halo = m // SUBLANES
    halo = tm // SUBLANES
    row_spec = pl.BlockSpec((tm, d), lambda i: (i, 0))

    def full(a):
        return pl.BlockSpec(a.shape, lambda i: (0,) * a.ndim)

    params = (norm, mu, wr, wk, wv, g1, g2, w1, w2, w0, a1, a2, a0, k_scale, k_a, r_k, ones_bd)
    return pl.pallas_call(
        functools.partial(_rwkv_pre_kernel, tm=tm, sub=sub, seq_len=seq_len),
        grid=(m // tm,),
        in_specs=[row_spec,
                  pl.BlockSpec((SUBLANES, d), lambda i: (jnp.maximum(i * halo - 1, 0), 0)),
                  pl.BlockSpec((SUBLANES, d), lambda i: (jnp.minimum((i + 1) * halo, n_halo - 1), 0)),
                  ] + [full(p) for p in params],
        out_specs=[row_spec] * 11,
        out_shape=[jax.ShapeDtypeStruct((m, d), F32)] * 11,
        compiler_params=pltpu.CompilerParams(
            dimension_semantics=("arbitrary",), vmem_limit_bytes=VMEM_LIMIT_BYTES),
        name="rwkv_pre",
    )(x2d, x2d, x2d, *params)


def _scan_operands(r_ref, v_ref, kk_ref, b_ref, kd_ref, lw_ref, bi, reverse):
    n = SCAN_CHUNK
    ti = lax.broadcasted_iota(jnp.int32, (n, n), 0)
    tj = lax.broadcasted_iota(jnp.int32, (n, n), 1)
    cum_mat = ((ti <= tj) if reverse else (ti >= tj)).astype(BF16)
    lw = lw_ref[bi]
    hi, mid, lo = _split3(lw)
    cum = (jnp.dot(cum_mat, hi, preferred_element_type=F32)
           + jnp.dot(cum_mat, mid, preferred_element_type=F32)
           + jnp.dot(cum_mat, lo, preferred_element_type=F32))
    e_incl = jnp.exp(cum)
    e_inv = jnp.exp(-cum)
    last = 0 if reverse else n - 1
    return dict(
        a=(-(kk_ref[bi] * jnp.exp(cum - lw))).astype(BF16),
        r=(r_ref[bi] * e_incl).astype(BF16),
        b=(b_ref[bi] * e_inv).astype(BF16),
        k=(kd_ref[bi] * e_inv).astype(BF16),
        v=v_ref[bi].astype(BF16),
        chunk_decay=e_incl[last:last + 1, :])


def _wkv_kernel(rf_ref, vf_ref, kkf_ref, bf_ref, kdf_ref, lwf_ref,
                rb_ref, vb_ref, kkb_ref, bb_ref, kdb_ref, lwb_ref,
                of_ref, ob_ref, state_ref):
    @pl.when(pl.program_id(0) == 0)
    def _():
        state_ref[...] = jnp.zeros_like(state_ref)

    n = SCAN_CHUNK
    batch, _, d = rf_ref.shape
    ci = lax.broadcasted_iota(jnp.int32, (n, LANES), 0)
    cj = lax.broadcasted_iota(jnp.int32, (n, LANES), 1)
    head0 = cj < HEAD_DIM
    cj = cj % HEAD_DIM
    same_head = (lax.broadcasted_iota(jnp.int32, (PAIR, LANES), 0) // HEAD_DIM
                 == lax.broadcasted_iota(jnp.int32, (PAIR, LANES), 1) // HEAD_DIM)

    def expand(x):
        zero = jnp.zeros_like(x)
        return jnp.concatenate([jnp.where(head0, x, zero), jnp.where(head0, zero, x)], axis=0)

    ops = [(_scan_operands(rf_ref, vf_ref, kkf_ref, bf_ref, kdf_ref, lwf_ref, bi, False),
            _scan_operands(rb_ref, vb_ref, kkb_ref, bb_ref, kdb_ref, lwb_ref, bi, True))
           for bi in range(batch)]

    def run_stages(chains):
        lanes = [slice(p * LANES, (p + 1) * LANES) for _, _, p in chains]
        strict = [(ci < cj) if rev else (ci > cj) for _, rev, _ in chains]
        incl = [(ci <= cj) if rev else (ci >= cj) for _, rev, _ in chains]
        idx = range(len(chains))

        def tile(name, i):
            bi, direction, _ = chains[i]
            return ops[bi][direction][name][:, lanes[i]]

        ar = [jnp.concatenate([tile("a", i), tile("r", i)], axis=0) for i in idx]
        bk = [jnp.concatenate([tile("b", i), tile("k", i)], axis=0) for i in idx]
        bk_x = [jnp.concatenate([expand(tile("b", i)), expand(tile("k", i))], axis=0) for i in idx]
        v_x = [expand(tile("v", i)) for i in idx]
        state = [state_ref[bi, direction, p] for bi, direction, p in chains]

        gram = [_mm_nt(ar[i], bk_x[i]) for i in idx]
        from_state = [_mm_nt(ar[i], state[i]) for i in idx]
        power = [jnp.where(strict[i], gram[i][:n, :PAIR], 0.0).astype(BF16) for i in idx]
        a_ak = [jnp.where(strict[i], gram[i][:n, PAIR:], 0.0) for i in idx]
        a_r = [jnp.where(jnp.concatenate([incl[i], incl[i]], axis=1), gram[i][n:, :], 0.0).astype(BF16)
               for i in idx]

        u = [from_state[i][:n] + _mm(a_ak[i], v_x[i]) for i in idx]
        steps = int(math.log2(n))
        for s in range(steps - 1):
            both = [jnp.dot(power[i],
                            jnp.concatenate([expand(power[i]), expand(u[i].astype(BF16))], axis=1),
                            preferred_element_type=F32) for i in idx]
            power = [x[:, :PAIR].astype(BF16) for x in both]
            u = [u[i] + both[i][:, PAIR:] for i in idx]
        u = [u[i] + jnp.dot(power[i], expand(u[i].astype(BF16)), preferred_element_type=F32)
             for i in idx]

        u16 = [x.astype(BF16) for x in u]
        out = [from_state[i][n:] + jnp.dot(a_r[i], jnp.concatenate([expand(u16[i]), v_x[i]], axis=0),
                                           preferred_element_type=F32) for i in idx]
        grown = [_mm_tn(jnp.concatenate([u16[i], tile("v", i)], axis=0), bk[i]) for i in idx]
        for i, (bi, direction, p) in enumerate(chains):
            (ob_ref if direction else of_ref)[bi, :, lanes[i]] = out[i]
            decay = ops[bi][direction]["chunk_decay"][:, lanes[i]]
            state_ref[bi, direction, p] = (state[i] + jnp.where(same_head, grown[i], 0.0)) * decay

    run_stages([(bi, direction, p) for bi in range(batch) for direction in range(2)
                for p in range(d // LANES)])


def _wkv_scan(r, v, kk, b0, kd0, lw0, b1, kd1, lw1, batch, seq_len):
    m, d = r.shape
    n_chunks = seq_len // SCAN_CHUNK
    fwd = pl.BlockSpec((batch, SCAN_CHUNK, d), lambda c: (0, c, 0))
    bwd = pl.BlockSpec((batch, SCAN_CHUNK, d), lambda c: (0, n_chunks - 1 - c, 0))
    rows = lambda a: a.reshape(batch, seq_len, d)
    r, v, kk, b0, kd0, lw0, b1, kd1, lw1 = map(rows, (r, v, kk, b0, kd0, lw0, b1, kd1, lw1))
    s0, s1 = pl.pallas_call(
        _wkv_kernel,
        grid=(n_chunks,),
        in_specs=[fwd] * 6 + [bwd] * 6,
        out_specs=[fwd, bwd],
        out_shape=[jax.ShapeDtypeStruct((batch, seq_len, d), F32)] * 2,
        scratch_shapes=[pltpu.VMEM((batch, 2, d // LANES, PAIR, LANES), F32)],
        compiler_params=pltpu.CompilerParams(
            dimension_semantics=("arbitrary",), vmem_limit_bytes=VMEM_LIMIT_BYTES),
        name="wkv_scan",
    )(r, v, kk, b0, kd0, lw0, r, v, kk, b1, kd1, lw1)
    return s0.reshape(m, d), s1.reshape(m, d)


def _rwkv_post_kernel(x_ref, s0_ref, s1_ref, bonus_ref, g_ref, lng_ref, lnb_ref, ones_ref,
                      wo_ref, o_ref):
    ones_bd = ones_ref[...]
    s = s0_ref[...] + s1_ref[...]
    oc = s - _head_sum(s, ones_bd) * (1.0 / HEAD_DIM)
    var = _head_sum(oc * oc, ones_bd) * (1.0 / HEAD_DIM)
    o = oc * lax.rsqrt(var + GN_EPS) * lng_ref[...] + lnb_ref[...] + bonus_ref[...]
    o_ref[...] = x_ref[...] + jnp.dot((o * g_ref[...]).astype(BF16), wo_ref[...],
                                      preferred_element_type=F32)


def _rwkv_post(x2d, s0, s1, bonus, g, ln_g, ln_b, ones_bd, wo, *, tm):
    m, d = x2d.shape
    row_spec = pl.BlockSpec((tm, d), lambda i: (i, 0))

    def full(a):
        return pl.BlockSpec(a.shape, lambda i: (0,) * a.ndim)

    return pl.pallas_call(
        _rwkv_post_kernel,
        grid=(m // tm,),
        in_specs=[row_spec] * 5 + [full(ln_g), full(ln_b), full(ones_bd), full(wo)],
        out_specs=row_spec,
        out_shape=jax.ShapeDtypeStruct((m, d), F32),
        compiler_params=pltpu.CompilerParams(
            dimension_semantics=("arbitrary",), vmem_limit_bytes=VMEM_LIMIT_BYTES),
        name="rwkv_post",
    )(x2d, s0, s1, bonus, g, ln_g, ln_b, ones_bd, wo)


def _ffn_kernel(x_ref, norm_ref, w1_ref, w2_ref, fnorm_ref, o_ref, *, th, final_norm):
    x = x_ref[...]
    xn = _rms(x, norm_ref[...]).astype(BF16)
    n_blocks = w1_ref.shape[1] // th

    def up(j):
        h = jnp.maximum(jnp.dot(xn, w1_ref[:, j * th:(j + 1) * th],
                                preferred_element_type=F32), 0.0)
        return (h * h).astype(BF16)

    acc = None
    act = up(0)
    for j in range(n_blocks):
        nxt = up(j + 1) if j + 1 < n_blocks else None
        part = jnp.dot(act, w2_ref[j * th:(j + 1) * th, :], preferred_element_type=F32)
        acc = part if acc is None else acc + part
        act = nxt
    y = x + acc
    o_ref[...] = _rms(y, fnorm_ref[...]) if final_norm else y


def _ffn(x2d, norm, w1, w2, fnorm, *, tm, th, final_norm):
    m, d = x2d.shape

    def full(a):
        return pl.BlockSpec(a.shape, lambda i: (0,) * a.ndim)

    return pl.pallas_call(
        functools.partial(_ffn_kernel, th=th, final_norm=final_norm),
        grid=(m // tm,),
        in_specs=[pl.BlockSpec((tm, d), lambda i: (i, 0)),
                  full(norm), full(w1), full(w2), full(fnorm)],
        out_specs=pl.BlockSpec((tm, d), lambda i: (i, 0)),
        out_shape=jax.ShapeDtypeStruct((m, d), F32),
        compiler_params=pltpu.CompilerParams(
            dimension_semantics=("arbitrary",), vmem_limit_bytes=VMEM_LIMIT_BYTES),
        name="ffn_final" if final_norm else "ffn",
    )(x2d, norm, w1, w2, fnorm)


def _gmlp_kernel(x_ref, norm_ref, win_ref, lng_ref, lnb_ref, ws_ref, bs_ref, wout_ref,
                 o_ref, gated_ref, *, tm, sub):
    n_sub = tm // sub
    n_chunks = sub // GMLP_CHUNK
    width = wout_ref.shape[0]
    hidden = [None] * n_sub

    def in_proj(i):
        xn = _rms(x_ref[i * sub:(i + 1) * sub, :], norm_ref[...]).astype(BF16)
        hidden[i] = jnp.dot(xn, win_ref[...], preferred_element_type=F32)

    def mix(i):
        h = hidden[i]
        h = 0.5 * h * (1.0 + lax.erf(h * (1.0 / math.sqrt(2.0))))
        u = h[:, :width]
        v = h[:, width:]
        mean = jnp.mean(v, axis=-1, keepdims=True)
        vc = v - mean
        var = jnp.mean(vc * vc, axis=-1, keepdims=True)
        v = (vc * lax.rsqrt(var + NORM_EPS) * lng_ref[...] + lnb_ref[...]).astype(BF16)
        for grp in range(width // GMLP_GROUP):
            cols = slice(grp * GMLP_GROUP, (grp + 1) * GMLP_GROUP)
            stacked = jnp.concatenate(
                [v[c * GMLP_CHUNK:(c + 1) * GMLP_CHUNK, cols] for c in range(n_chunks)], axis=1)
            mixed = jnp.dot(ws_ref[grp], stacked, preferred_element_type=F32)
            for c in range(n_chunks):
                rows = slice(c * GMLP_CHUNK, (c + 1) * GMLP_CHUNK)
                part = mixed[:, c * GMLP_GROUP:(c + 1) * GMLP_GROUP] + bs_ref[:, cols]
                gated_ref[i, rows, cols] = (u[rows, cols] * part).astype(BF16)

    def out_proj(i):
        rows = slice(i * sub, (i + 1) * sub)
        o_ref[rows, :] = x_ref[rows, :] + jnp.dot(gated_ref[i], wout_ref[...],
                                                  preferred_element_type=F32)

    in_proj(0)
    for i in range(n_sub):
        if i + 1 < n_sub:
            in_proj(i + 1)
        mix(i)
        if i >= 1:
            out_proj(i - 1)
    out_proj(n_sub - 1)


def _gmlp(x2d, norm, w_in, ln_g, ln_b, w_s, b_s_full, w_out, *, tm, sub):
    m, d = x2d.shape
    width = w_out.shape[0]
    row_spec = pl.BlockSpec((tm, d), lambda i: (i, 0))

    def full(a):
        return pl.BlockSpec(a.shape, lambda i: (0,) * a.ndim)

    params = (norm, w_in, ln_g, ln_b, w_s, b_s_full, w_out)
    return pl.pallas_call(
        functools.partial(_gmlp_kernel, tm=tm, sub=sub),
        grid=(m // tm,),
        in_specs=[row_spec] + [full(p) for p in params],
        out_specs=row_spec,
        out_shape=jax.ShapeDtypeStruct((m, d), F32),
        scratch_shapes=[pltpu.VMEM((tm // sub, sub, width), BF16)],
        compiler_params=pltpu.CompilerParams(
            dimension_semantics=("arbitrary",), vmem_limit_bytes=VMEM_LIMIT_BYTES),
        name="gmlp",
    )(x2d, *params)


def _block_diag2(w):
    z = jnp.zeros_like(w[0])
    return jnp.concatenate([jnp.concatenate([w[0], z], axis=1),
                            jnp.concatenate([z, w[1]], axis=1)], axis=0)


def kernel(x, mix_norm, ffn_norm, final_norm, rwkv_mu, rwkv_wr, rwkv_wk, rwkv_wv, rwkv_wo, rwkv_w0, rwkv_w1, rwkv_w2, rwkv_a0, rwkv_a1, rwkv_a2, rwkv_g1, rwkv_g2, rwkv_k_k, rwkv_k_a, rwkv_r_k, rwkv_ln_g, rwkv_ln_b, gmlp_w_in, gmlp_ln_g, gmlp_ln_b, gmlp_w_s, gmlp_b_s, gmlp_w_out, ffn_w1, ffn_w2):
    batch, seq_len, d = x.shape
    depth = mix_norm.shape[0]
    m = batch * seq_len
    assert d % LANES == 0 and seq_len % SCAN_CHUNK == 0 and seq_len % GMLP_CHUNK == 0
    tm_pre = min(256, seq_len)
    sub_pre = min(128, tm_pre)
    tm_post = min(512, seq_len)
    tm_ffn = min(1024, m)
    tm_gmlp = min(1024, seq_len)
    sub_gmlp = min(256, tm_gmlp)
    th_ffn = 1024

    lane = jnp.arange(MXU_TILE) // HEAD_DIM
    ones_bd = (lane[:, None] == lane[None, :]).astype(BF16)
    row = lambda a: a.reshape(1, -1).astype(F32)
    cat_dirs = lambda w: jnp.concatenate([w[0], w[1]], axis=-1)

    h = x.reshape(m, d)
    for layer in range(depth):
        j = layer // 2
        norm = row(mix_norm[layer])
        if layer % 2 == 0:
            outs = _rwkv_pre(
                h, seq_len, norm, rwkv_mu[j],
                rwkv_wr[j].astype(BF16), rwkv_wk[j].astype(BF16), rwkv_wv[j].astype(BF16),
                (0.5 * rwkv_g1[j]).astype(BF16), rwkv_g2[j].astype(BF16),
                cat_dirs(rwkv_w1[j]).astype(BF16), _block_diag2(0.5 * rwkv_w2[j]).astype(BF16),
                row(cat_dirs(0.5 * rwkv_w0[j])),
                cat_dirs(rwkv_a1[j]).astype(BF16), _block_diag2(0.5 * rwkv_a2[j]).astype(BF16),
                row(cat_dirs(0.5 * rwkv_a0[j])),
                row(rwkv_k_k[j]), row(rwkv_k_a[j]), row(rwkv_r_k[j]), ones_bd, tm=tm_pre,
                sub=sub_pre)
            r, v, kk, g, bonus, b0, kd0, lw0, b1, kd1, lw1 = outs
            s0, s1 = _wkv_scan(r, v, kk, b0, kd0, lw0, b1, kd1, lw1, batch, seq_len)
            h = _rwkv_post(h, s0, s1, bonus, g, row(rwkv_ln_g[j]), row(rwkv_ln_b[j]), ones_bd,
                           rwkv_wo[j].astype(BF16), tm=tm_post)
        else:
            b_full = jnp.repeat(jnp.transpose(gmlp_b_s[j]), GMLP_GROUP, axis=1)
            h = _gmlp(h, norm, gmlp_w_in[j].astype(BF16), row(gmlp_ln_g[j]), row(gmlp_ln_b[j]),
                      gmlp_w_s[j].astype(BF16), b_full, gmlp_w_out[j].astype(BF16), tm=tm_gmlp,
                      sub=sub_gmlp)
        last = layer == depth - 1
        h = _ffn(h, row(ffn_norm[layer]), ffn_w1[layer].astype(BF16), ffn_w2[layer].astype(BF16),
                 row(final_norm), tm=tm_ffn, th=th_ffn, final_norm=last)
    return h.reshape(batch, seq_len, d)
```

```python
import functools
import math

import jax
import jax.numpy as jnp
from jax import lax
from jax.experimental import pallas as pl
from jax.experimental.pallas import tpu as pltpu

F32 = jnp.float32
BF16 = jnp.bfloat16

HEAD_DIM = 64
GN_EPS = 64e-5
NORM_EPS = 1e-5
GMLP_CHUNK = 128
GMLP_GROUP = 128

LANES = 128
SUBLANES = 8
MXU_TILE = 256
HEADS_PER_TILE = LANES // HEAD_DIM
VMEM_LIMIT_BYTES = 56 * 1024 * 1024

SCAN_CHUNK = 64
PAIR = HEADS_PER_TILE * SCAN_CHUNK


def _mm(a, b):
    return jnp.dot(a.astype(BF16), b.astype(BF16), preferred_element_type=F32)


def _mm_nt(a, b):
    return lax.dot_general(a.astype(BF16), b.astype(BF16), (((1,), (1,)), ((), ())),
                           preferred_element_type=F32)


def _mm_tn(a, b):
    return lax.dot_general(a.astype(BF16), b.astype(BF16), (((0,), (0,)), ((), ())),
                           preferred_element_type=F32)


def _sigmoid_of_twice(half_x):
    return 0.5 * jnp.tanh(half_x) + 0.5


def _rms(x, g):
    return x * lax.rsqrt(jnp.mean(x * x, axis=-1, keepdims=True) + NORM_EPS) * g


def _head_sum(x, ones_bd):
    width = ones_bd.shape[0]
    xb = x.astype(BF16)
    cols = [jnp.dot(xb[:, j * width:(j + 1) * width], ones_bd, preferred_element_type=F32)
            for j in range(x.shape[1] // width)]
    return jnp.concatenate(cols, axis=1)


def _rwkv_pre_kernel(x_ref, xprev_ref, xnext_ref, norm_ref, mu_ref,
                     wr_ref, wk_ref, wv_ref, g1_ref, g2_ref,
                     w1_ref, w2_ref, w0_ref, a1_ref, a2_ref, a0_ref,
                     kscale_ref, ka_ref, rk_ref, ones_ref,
                     r_out, v_out, kk_out, g_out, bonus_out,
                     b0_out, kd0_out, lw0_out, b1_out, kd1_out, lw1_out,
                     *, tm, sub, seq_len):
    i = pl.program_id(0)
    d = x_ref.shape[1]
    gain = norm_ref[...]
    xn = _rms(x_ref[...], gain)
    at_start = (i * tm) % seq_len == 0
    at_end = ((i + 1) * tm) % seq_len == 0
    edge_prev = jnp.where(at_start, 0.0, _rms(xprev_ref[SUBLANES - 1:SUBLANES, :], gain))
    edge_next = jnp.where(at_end, 0.0, _rms(xnext_ref[0:1, :], gain))
    row = lax.broadcasted_iota(jnp.int32, (tm, 1), 0)
    x_prev = jnp.where(row == 0, edge_prev, pltpu.roll(xn, 1, axis=0))
    x_next = jnp.where(row == tm - 1, edge_next, pltpu.roll(xn, tm - 1, axis=0))
    xx = 0.5 * (x_prev + x_next) - xn
    ones_bd = ones_ref[...]
    k_a = ka_ref[...]

    def project(rows):
        def mixed(j):
            return (xn[rows] + xx[rows] * mu_ref[j:j + 1, :]).astype(BF16)

        dot = functools.partial(jnp.dot, preferred_element_type=F32)
        w_mid = dot(mixed(1), w1_ref[...])
        a_mid = dot(mixed(4), a1_ref[...])
        g_mid = dot(mixed(5), g1_ref[...])
        zw = _mm(jnp.tanh(w_mid), w2_ref[...]) + w0_ref[...]
        za = _mm(a_mid, a2_ref[...]) + a0_ref[...]
        g_out[rows, :] = _mm(_sigmoid_of_twice(g_mid), g2_ref[...])
        k = dot(mixed(2), wk_ref[...])
        r = dot(mixed(0), wr_ref[...])
        v = dot(mixed(3), wv_ref[...])
        return zw, za, k, r, v

    def epilogue(rows, zw, za, k, r, v):
        r_out[rows, :] = r
        v_out[rows, :] = v
        kk = k * kscale_ref[...]
        kk = kk * lax.rsqrt(jnp.maximum(_head_sum(kk * kk, ones_bd), 1e-24))
        kk_out[rows, :] = kk
        k_keep = k * (1.0 - k_a)
        k_gated = k * k_a
        kd = []
        for direction, (b_out, kd_out, lw_out) in enumerate(
                ((b0_out, kd0_out, lw0_out), (b1_out, kd1_out, lw1_out))):
            sl = slice(direction * d, (direction + 1) * d)
            half_c = -0.5 * math.exp(-0.5)
            lw_out[rows, :] = half_c * jnp.tanh(zw[:, sl]) + half_c
            a_gate = _sigmoid_of_twice(za[:, sl])
            kd.append(k_keep + k_gated * a_gate)
            kd_out[rows, :] = kd[direction]
            b_out[rows, :] = kk * a_gate
        bonus_out[rows, :] = _head_sum(r * (kd[0] + kd[1]) * rk_ref[...], ones_bd) * v

    sub_rows = [slice(s * sub, (s + 1) * sub) for s in range(tm // sub)]
    projected = [project(rows) for rows in sub_rows]
    for rows, outs in zip(sub_rows, projected):
        epilogue(rows, *outs)


def _rwkv_pre(x2d, seq_len, norm, mu, wr, wk, wv, g1, g2, w1, w2, w0, a1, a2, a0,
              k_scale, k_a, r_k, ones_bd, *, tm, sub):
    m, d = x2d.shape
    n_halo = m // SUBLANES
    halo = tm // SUBLANES
    row_spec = pl.BlockSpec((tm, d), lambda i: (i, 0))

    def full(a):
        return pl.BlockSpec(a.shape, lambda i: (0,) * a.ndim)

    params = (norm, mu, wr, wk, wv, g1, g2, w1, w2, w0, a1, a2, a0, k_scale, k_a, r_k, ones_bd)
    return pl.pallas_call(
        functools.partial(_rwkv_pre_kernel, tm=tm, sub=sub, seq_len=seq_len),
        grid=(m // tm,),
        in_specs=[row_spec,
                  pl.BlockSpec((SUBLANES, d), lambda i: (jnp.maximum(i * halo - 1, 0), 0)),
                  pl.BlockSpec((SUBLANES, d), lambda i: (jnp.minimum((i + 1) * halo, n_halo - 1), 0)),
                  ] + [full(p) for p in params],
        out_specs=[row_spec] * 11,
        out_shape=[jax.ShapeDtypeStruct((m, d), F32)] * 11,
        compiler_params=pltpu.CompilerParams(
            dimension_semantics=("arbitrary",), vmem_limit_bytes=VMEM_LIMIT_BYTES),
        name="rwkv_pre",
    )(x2d, x2d, x2d, *params)


def _scan_operands(r_ref, v_ref, kk_ref, b_ref, kd_ref, lw_ref, bi, reverse):
    n = SCAN_CHUNK
    ti = lax.broadcasted_iota(jnp.int32, (n, n), 0)
    tj = lax.broadcasted_iota(jnp.int32, (n, n), 1)
    cum_mat = ((ti <= tj) if reverse else (ti >= tj)).astype(BF16)
    lw = lw_ref[bi]
    hi = lw.astype(BF16)
    lo = (lw - hi.astype(F32)).astype(BF16)
    cum = jnp.dot(jnp.concatenate([cum_mat, cum_mat], axis=1), jnp.concatenate([hi, lo], axis=0),
                  preferred_element_type=F32)
    e_incl = jnp.exp(cum)
    e_inv = jnp.exp(-cum)
    last = 0 if reverse else n - 1
    return dict(
        a=(-(kk_ref[bi] * jnp.exp(cum - lw))).astype(BF16),
        r=(r_ref[bi] * e_incl).astype(BF16),
        b=(b_ref[bi] * e_inv).astype(BF16),
        k=(kd_ref[bi] * e_inv).astype(BF16),
        v=v_ref[bi].astype(BF16),
        chunk_decay=e_incl[last:last + 1, :])


def _wkv_kernel(rf_ref, vf_ref, kkf_ref, bf_ref, kdf_ref, lwf_ref,
                rb_ref, vb_ref, kkb_ref, bb_ref, kdb_ref, lwb_ref,
                of_ref, ob_ref, state_ref):
    @pl.when(pl.program_id(0) == 0)
    def _():
        state_ref[...] = jnp.zeros_like(state_ref)

    n = SCAN_CHUNK
    batch, _, d = rf_ref.shape
    ci = lax.broadcasted_iota(jnp.int32, (n, LANES), 0)
    cj = lax.broadcasted_iota(jnp.int32, (n, LANES), 1)
    head0 = cj < HEAD_DIM
    cj = cj % HEAD_DIM
    same_head = (lax.broadcasted_iota(jnp.int32, (PAIR, LANES), 0) // HEAD_DIM
                 == lax.broadcasted_iota(jnp.int32, (PAIR, LANES), 1) // HEAD_DIM)

    def expand(x):
        zero = jnp.zeros_like(x)
        return jnp.concatenate([jnp.where(head0, x, zero), jnp.where(head0, zero, x)], axis=0)

    ops = [(_scan_operands(rf_ref, vf_ref, kkf_ref, bf_ref, kdf_ref, lwf_ref, bi, False),
            _scan_operands(rb_ref, vb_ref, kkb_ref, bb_ref, kdb_ref, lwb_ref, bi, True))
           for bi in range(batch)]

    def run_stages(chains):
        lanes = [slice(p * LANES, (p + 1) * LANES) for _, _, p in chains]
        strict = [(ci < cj) if rev else (ci > cj) for _, rev, _ in chains]
        incl = [(ci <= cj) if rev else (ci >= cj) for _, rev, _ in chains]
        idx = range(len(chains))

        def tile(name, i):
            bi, direction, _ = chains[i]
            return ops[bi][direction][name][:, lanes[i]]

        ar = [jnp.concatenate([tile("a", i), tile("r", i)], axis=0) for i in idx]
        bk = [jnp.concatenate([tile("b", i), tile("k", i)], axis=0) for i in idx]
        bk_x = [jnp.concatenate([expand(tile("b", i)), expand(tile("k", i))], axis=0) for i in idx]
        v_x = [expand(tile("v", i)) for i in idx]
        state = [state_ref[bi, direction, p] for bi, direction, p in chains]

        gram = [_mm_nt(ar[i], bk_x[i]) for i in idx]
        from_state = [_mm_nt(ar[i], state[i]) for i in idx]
        power = [jnp.where(strict[i], gram[i][:n, :PAIR], 0.0).astype(BF16) for i in idx]
        a_ak = [jnp.where(strict[i], gram[i][:n, PAIR:], 0.0) for i in idx]
        a_r = [jnp.where(jnp.concatenate([incl[i], incl[i]], axis=1), gram[i][n:, :], 0.0).astype(BF16)
               for i in idx]

        u = [from_state[i][:n] + _mm(a_ak[i], v_x[i]) for i in idx]
        steps = int(math.log2(n))
        for s in range(steps - 1):
            both = [jnp.dot(power[i],
                            jnp.concatenate([expand(power[i]), expand(u[i].astype(BF16))], axis=1),
                            preferred_element_type=F32) for i in idx]
            power = [x[:, :PAIR].astype(BF16) for x in both]
            u = [u[i] + both[i][:, PAIR:] for i in idx]
        u = [u[i] + jnp.dot(power[i], expand(u[i].astype(BF16)), preferred_element_type=F32)
             for i in idx]

        u16 = [x.astype(BF16) for x in u]
        out = [from_state[i][n:] + jnp.dot(a_r[i], jnp.concatenate([expand(u16[i]), v_x[i]], axis=0),
                                           preferred_element_type=F32) for i in idx]
        grown = [_mm_tn(jnp.concatenate([u16[i], tile("v", i)], axis=0), bk[i]) for i in idx]
        for i, (bi, direction, p) in enumerate(chains):
            (ob_ref if direction else of_ref)[bi, :, lanes[i]] = out[i]
            decay = ops[bi][direction]["chunk_decay"][:, lanes[i]]
            state_ref[bi, direction, p] = (state[i] + jnp.where(same_head, grown[i], 0.0)) * decay

    run_stages([(bi, direction, p) for bi in range(batch) for direction in range(2)
                for p in range(d // LANES)])


def _wkv_scan(r, v, kk, b0, kd0, lw0, b1, kd1, lw1, batch, seq_len):
    m, d = r.shape
    n_chunks = seq_len // SCAN_CHUNK
    fwd = pl.BlockSpec((batch, SCAN_CHUNK, d), lambda c: (0, c, 0))
    bwd = pl.BlockSpec((batch, SCAN_CHUNK, d), lambda c: (0, n_chunks - 1 - c, 0))
    rows = lambda a: a.reshape(batch, seq_len, d)
    r, v, kk, b0, kd0, lw0, b1, kd1, lw1 = map(rows, (r, v, kk, b0, kd0, lw0, b1, kd1, lw1))
    s0, s1 = pl.pallas_call(
        _wkv_kernel,
        grid=(n_chunks,),
        in_specs=[fwd] * 6 + [bwd] * 6,
        out_specs=[fwd, bwd],
        out_shape=[jax.ShapeDtypeStruct((batch, seq_len, d), F32)] * 2,
        scratch_shapes=[pltpu.VMEM((batch, 2, d // LANES, PAIR, LANES), F32)],
        compiler_params=pltpu.CompilerParams(
            dimension_semantics=("arbitrary",), vmem_limit_bytes=VMEM_LIMIT_BYTES),
        name="wkv_scan",
    )(r, v, kk, b0, kd0, lw0, r, v, kk, b1, kd1, lw1)
    return s0.reshape(m, d), s1.reshape(m, d)


def _ffn_rows(x, norm_ref, w1_ref, w2_ref, th, side):
    xn = _rms(x, norm_ref[...]).astype(BF16)
    n_blocks = w1_ref.shape[1] // th

    def up(j):
        h = jnp.maximum(jnp.dot(xn, w1_ref[:, j * th:(j + 1) * th],
                                preferred_element_type=F32), 0.0)
        side()
        return (h * h).astype(BF16)

    acc = None
    act = up(0)
    for j in range(n_blocks):
        nxt = up(j + 1) if j + 1 < n_blocks else None
        part = jnp.dot(act, w2_ref[j * th:(j + 1) * th, :], preferred_element_type=F32)
        side()
        acc = part if acc is None else acc + part
        act = nxt
    return x + acc


def _rwkv_out_ffn_kernel(x_ref, s0_ref, s1_ref, bonus_ref, g_ref, lng_ref, lnb_ref, ones_ref,
                         wo_ref, norm_ref, w1_ref, w2_ref, o_ref, mixed_ref, *, th):
    step = pl.program_id(0)

    @pl.when(step == 0)
    def _():
        mixed_ref[1] = jnp.zeros(mixed_ref.shape[1:], mixed_ref.dtype)

    def output_stage(slot):
        ones_bd = ones_ref[...]
        keep = {}

        def add_directions():
            keep["s"] = s0_ref[...] + s1_ref[...]

        def head_mean():
            keep["mean"] = _head_sum(keep["s"], ones_bd) * (1.0 / HEAD_DIM)

        def centre():
            keep["oc"] = keep.pop("s") - keep.pop("mean")
            keep["sq"] = keep["oc"] * keep["oc"]

        def head_var():
            keep["var"] = _head_sum(keep.pop("sq"), ones_bd) * (1.0 / HEAD_DIM)

        def normalise():
            o = (keep.pop("oc") * lax.rsqrt(keep.pop("var") + GN_EPS) * lng_ref[...]
                 + lnb_ref[...] + bonus_ref[...])
            keep["gated"] = (o * g_ref[...]).astype(BF16)

        def project():
            mixed_ref[slot] = x_ref[...] + jnp.dot(keep.pop("gated"), wo_ref[...],
                                                   preferred_element_type=F32)

        return [add_directions, head_mean, centre, head_var, normalise, project]

    for prep_slot in range(2):
        @pl.when(step % 2 == prep_slot)
        def _():
            pieces = output_stage(prep_slot)

            def side():
                if pieces:
                    pieces.pop(0)()

            o_ref[...] = _ffn_rows(mixed_ref[1 - prep_slot], norm_ref, w1_ref, w2_ref, th, side)
            while pieces:
                side()


def _rwkv_out_ffn(x2d, s0, s1, bonus, g, ln_g, ln_b, ones_bd, wo, norm, w1, w2, *, tm, th):
    m, d = x2d.shape
    last = m // tm - 1
    rows_in = pl.BlockSpec((tm, d), lambda i: (jnp.minimum(i, last), 0))
    rows_out = pl.BlockSpec((tm, d), lambda i: (jnp.maximum(i - 1, 0), 0))

    def full(a):
        return pl.BlockSpec(a.shape, lambda i: (0,) * a.ndim)

    params = (ln_g, ln_b, ones_bd, wo, norm, w1, w2)
    return pl.pallas_call(
        functools.partial(_rwkv_out_ffn_kernel, th=th),
        grid=(m // tm + 1,),
        in_specs=[rows_in] * 5 + [full(p) for p in params],
        out_specs=rows_out,
        out_shape=jax.ShapeDtypeStruct((m, d), F32),
        scratch_shapes=[pltpu.VMEM((2, tm, d), F32)],
        compiler_params=pltpu.CompilerParams(
            dimension_semantics=("arbitrary",), vmem_limit_bytes=VMEM_LIMIT_BYTES),
        name="rwkv_out_ffn",
    )(x2d, s0, s1, bonus, g, *params)


def _ffn_kernel(x_ref, norm_ref, w1_ref, w2_ref, fnorm_ref, o_ref, *, th, final_norm):
    y = _ffn_rows(x_ref[...], norm_ref, w1_ref, w2_ref, th, lambda: None)
    o_ref[...] = _rms(y, fnorm_ref[...]) if final_norm else y


def _ffn(x2d, norm, w1, w2, fnorm, *, tm, th, final_norm):
    m, d = x2d.shape

    def full(a):
        return pl.BlockSpec(a.shape, lambda i: (0,) * a.ndim)

    return pl.pallas_call(
        functools.partial(_ffn_kernel, th=th, final_norm=final_norm),
        grid=(m // tm,),
        in_specs=[pl.BlockSpec((tm, d), lambda i: (i, 0)),
                  full(norm), full(w1), full(w2), full(fnorm)],
        out_specs=pl.BlockSpec((tm, d), lambda i: (i, 0)),
        out_shape=jax.ShapeDtypeStruct((m, d), F32),
        compiler_params=pltpu.CompilerParams(
            dimension_semantics=("arbitrary",), vmem_limit_bytes=VMEM_LIMIT_BYTES),
        name="ffn_final" if final_norm else "ffn",
    )(x2d, norm, w1, w2, fnorm)


def _gmlp_kernel(x_ref, norm_ref, win_ref, lng_ref, lnb_ref, ws_ref, bs_ref, wout_ref,
                 o_ref, gated_ref, *, tm, sub):
    n_sub = tm // sub
    n_chunks = sub // GMLP_CHUNK
    width = wout_ref.shape[0]
    hidden = [None] * n_sub

    def in_proj(i):
        xn = _rms(x_ref[i * sub:(i + 1) * sub, :], norm_ref[...]).astype(BF16)
        hidden[i] = jnp.dot(xn, win_ref[...], preferred_element_type=F32)

    def mix(i):
        h = hidden[i]
        h = 0.5 * h * (1.0 + lax.erf(h * (1.0 / math.sqrt(2.0))))
        u = h[:, :width]
        v = h[:, width:]
        mean = jnp.mean(v, axis=-1, keepdims=True)
        vc = v - mean
        var = jnp.mean(vc * vc, axis=-1, keepdims=True)
        v = (vc * lax.rsqrt(var + NORM_EPS) * lng_ref[...] + lnb_ref[...]).astype(BF16)
        for grp in range(width // GMLP_GROUP):
            cols = slice(grp * GMLP_GROUP, (grp + 1) * GMLP_GROUP)
            stacked = jnp.concatenate(
                [v[c * GMLP_CHUNK:(c + 1) * GMLP_CHUNK, cols] for c in range(n_chunks)], axis=1)
            mixed = jnp.dot(ws_ref[grp], stacked, preferred_element_type=F32)
            for c in range(n_chunks):
                rows = slice(c * GMLP_CHUNK, (c + 1) * GMLP_CHUNK)
                part = mixed[:, c * GMLP_GROUP:(c + 1) * GMLP_GROUP] + bs_ref[:, cols]
                gated_ref[i, rows, cols] = (u[rows, cols] * part).astype(BF16)

    def out_proj(i):
        rows = slice(i * sub, (i + 1) * sub)
        o_ref[rows, :] = x_ref[rows, :] + jnp.dot(gated_ref[i], wout_ref[...],
                                                  preferred_element_type=F32)

    in_proj(0)
    for i in range(n_sub):
        if i + 1 < n_sub:
            in_proj(i + 1)
        mix(i)
        if i >= 1:
            out_proj(i - 1)
    out_proj(n_sub - 1)


def _gmlp(x2d, norm, w_in, ln_g, ln_b, w_s, b_s_full, w_out, *, tm, sub):
    m, d = x2d.shape
    width = w_out.shape[0]
    row_spec = pl.BlockSpec((tm, d), lambda i: (i, 0))

    def full(a):
        return pl.BlockSpec(a.shape, lambda i: (0,) * a.ndim)

    params = (norm, w_in, ln_g, ln_b, w_s, b_s_full, w_out)
    return pl.pallas_call(
        functools.partial(_gmlp_kernel, tm=tm, sub=sub),
        grid=(m // tm,),
        in_specs=[row_spec] + [full(p) for p in params],
        out_specs=row_spec,
        out_shape=jax.ShapeDtypeStruct((m, d), F32),
        scratch_shapes=[pltpu.VMEM((tm // sub, sub, width), BF16)],
        compiler_params=pltpu.CompilerParams(
            dimension_semantics=("arbitrary",), vmem_limit_bytes=VMEM_LIMIT_BYTES),
        name="gmlp",
    )(x2d, *params)


def _block_diag2(w):
    z = jnp.zeros_like(w[0])
    return jnp.concatenate([jnp.concatenate([w[0], z], axis=1),
                            jnp.concatenate([z, w[1]], axis=1)], axis=0)


def kernel(x, mix_norm, ffn_norm, final_norm, rwkv_mu, rwkv_wr, rwkv_wk, rwkv_wv, rwkv_wo, rwkv_w0, rwkv_w1, rwkv_w2, rwkv_a0, rwkv_a1, rwkv_a2, rwkv_g1, rwkv_g2, rwkv_k_k, rwkv_k_a, rwkv_r_k, rwkv_ln_g, rwkv_ln_b, gmlp_w_in, gmlp_ln_g, gmlp_ln_b, gmlp_w_s, gmlp_b_s, gmlp_w_out, ffn_w1, ffn_w2):
    batch, seq_len, d = x.shape
    depth = mix_norm.shape[0]
    m = batch * seq_len
    assert d % LANES == 0 and seq_len % SCAN_CHUNK == 0 and seq_len % GMLP_CHUNK == 0
    tm_pre = min(256, seq_len)
    sub_pre = min(128, tm_pre)
    tm_post = min(512, seq_len)
    tm_ffn = min(1024, m)
    tm_gmlp = min(1024, seq_len)
    sub_gmlp = min(256, tm_gmlp)
    th_ffn = 1024

    lane = jnp.arange(MXU_TILE) // HEAD_DIM
    ones_bd = (lane[:, None] == lane[None, :]).astype(BF16)
    row = lambda a: a.reshape(1, -1).astype(F32)
    cat_dirs = lambda w: jnp.concatenate([w[0], w[1]], axis=-1)

    h = x.reshape(m, d)
    for layer in range(depth):
        j = layer // 2
        norm = row(mix_norm[layer])
        if layer % 2 == 0:
            outs = _rwkv_pre(
                h, seq_len, norm, rwkv_mu[j],
                rwkv_wr[j].astype(BF16), rwkv_wk[j].astype(BF16), rwkv_wv[j].astype(BF16),
                (0.5 * rwkv_g1[j]).astype(BF16), rwkv_g2[j].astype(BF16),
                cat_dirs(rwkv_w1[j]).astype(BF16), _block_diag2(0.5 * rwkv_w2[j]).astype(BF16),
                row(cat_dirs(0.5 * rwkv_w0[j])),
                cat_dirs(rwkv_a1[j]).astype(BF16), _block_diag2(0.5 * rwkv_a2[j]).astype(BF16),
                row(cat_dirs(0.5 * rwkv_a0[j])),
                row(rwkv_k_k[j]), row(rwkv_k_a[j]), row(rwkv_r_k[j]), ones_bd, tm=tm_pre,
                sub=sub_pre)
            r, v, kk, g, bonus, b0, kd0, lw0, b1, kd1, lw1 = outs
            s0, s1 = _wkv_scan(r, v, kk, b0, kd0, lw0, b1, kd1, lw1, batch, seq_len)
            assert layer < depth - 1
            h = _rwkv_out_ffn(h, s0, s1, bonus, g, row(rwkv_ln_g[j]), row(rwkv_ln_b[j]), ones_bd,
                              rwkv_wo[j].astype(BF16), row(ffn_norm[layer]),
                              ffn_w1[layer].astype(BF16), ffn_w2[layer].astype(BF16),
                              tm=tm_post, th=th_ffn)
            continue
        else:
            b_full = jnp.repeat(jnp.transpose(gmlp_b_s[j]), GMLP_GROUP, axis=1)
            h = _gmlp(h, norm, gmlp_w_in[j].astype(BF16), row(gmlp_ln_g[j]), row(gmlp_ln_b[j]),
                      gmlp_w_s[j].astype(BF16), b_full, gmlp_w_out[j].astype(BF16), tm=tm_gmlp,
                      sub=sub_gmlp)
        last = layer == depth - 1
        h = _ffn(h, row(ffn_norm[layer]), ffn_w1[layer].astype(BF16), ffn_w2[layer].astype(BF16),
                 row(final_norm), tm=tm_ffn, th=th_ffn, final_norm=last)
    return h.reshape(batch, seq_len, d)
```

```python
import functools
import math

import jax
import jax.numpy as jnp
from jax import lax
from jax.experimental import pallas as pl
from jax.experimental.pallas import tpu as pltpu

F32 = jnp.float32
BF16 = jnp.bfloat16

HEAD_DIM = 64
GN_EPS = 64e-5
NORM_EPS = 1e-5
GMLP_CHUNK = 128
GMLP_GROUP = 128

LANES = 128
SUBLANES = 8
MXU_TILE = 256
HEADS_PER_TILE = LANES // HEAD_DIM
VMEM_LIMIT_BYTES = 56 * 1024 * 1024

SCAN_CHUNK = 64
PAIR = HEADS_PER_TILE * SCAN_CHUNK


def _mm(a, b):
    return jnp.dot(a.astype(BF16), b.astype(BF16), preferred_element_type=F32)


def _mm_nt(a, b):
    return lax.dot_general(a.astype(BF16), b.astype(BF16), (((1,), (1,)), ((), ())),
                           preferred_element_type=F32)


def _mm_tn(a, b):
    return lax.dot_general(a.astype(BF16), b.astype(BF16), (((0,), (0,)), ((), ())),
                           preferred_element_type=F32)


def _sigmoid_of_twice(half_x):
    return 0.5 * jnp.tanh(half_x) + 0.5


def _rms(x, g):
    return x * lax.rsqrt(jnp.mean(x * x, axis=-1, keepdims=True) + NORM_EPS) * g


def _head_sum(x, ones_bd):
    width = ones_bd.shape[0]
    xb = x.astype(BF16)
    cols = [jnp.dot(xb[:, j * width:(j + 1) * width], ones_bd, preferred_element_type=F32)
            for j in range(x.shape[1] // width)]
    return jnp.concatenate(cols, axis=1)


def _rwkv_pre_kernel(x_ref, xprev_ref, xnext_ref, norm_ref, mu_ref,
                     wr_ref, wk_ref, wv_ref, g1_ref, g2_ref,
                     w1_ref, w2_ref, w0_ref, a1_ref, a2_ref, a0_ref,
                     kscale_ref, ka_ref, rk_ref, ones_ref,
                     r_out, v_out, kk_out, g_out, bonus_out,
                     b0_out, kd0_out, lw0_out, b1_out, kd1_out, lw1_out,
                     *, tm, sub, seq_len):
    i = pl.program_id(0)
    d = x_ref.shape[1]
    gain = norm_ref[...]
    xn = _rms(x_ref[...], gain)
    at_start = (i * tm) % seq_len == 0
    at_end = ((i + 1) * tm) % seq_len == 0
    edge_prev = jnp.where(at_start, 0.0, _rms(xprev_ref[SUBLANES - 1:SUBLANES, :], gain))
    edge_next = jnp.where(at_end, 0.0, _rms(xnext_ref[0:1, :], gain))
    row = lax.broadcasted_iota(jnp.int32, (tm, 1), 0)
    x_prev = jnp.where(row == 0, edge_prev, pltpu.roll(xn, 1, axis=0))
    x_next = jnp.where(row == tm - 1, edge_next, pltpu.roll(xn, tm - 1, axis=0))
    xx = 0.5 * (x_prev + x_next) - xn
    ones_bd = ones_ref[...]
    k_a = ka_ref[...]

    def project(rows):
        def mixed(j):
            return (xn[rows] + xx[rows] * mu_ref[j:j + 1, :]).astype(BF16)

        dot = functools.partial(jnp.dot, preferred_element_type=F32)
        w_mid = dot(mixed(1), w1_ref[...])
        a_mid = dot(mixed(4), a1_ref[...])
        g_mid = dot(mixed(5), g1_ref[...])
        zw = _mm(jnp.tanh(w_mid), w2_ref[...]) + w0_ref[...]
        za = _mm(a_mid, a2_ref[...]) + a0_ref[...]
        g_out[rows, :] = _mm(_sigmoid_of_twice(g_mid), g2_ref[...])
        k = dot(mixed(2), wk_ref[...])
        r = dot(mixed(0), wr_ref[...])
        v = dot(mixed(3), wv_ref[...])
        return zw, za, k, r, v

    def epilogue(rows, zw, za, k, r, v):
        r_out[rows, :] = r.astype(r_out.dtype)
        v_out[rows, :] = v.astype(v_out.dtype)
        kk = k * kscale_ref[...]
        kk = kk * lax.rsqrt(jnp.maximum(_head_sum(kk * kk, ones_bd), 1e-24))
        kk_out[rows, :] = kk.astype(kk_out.dtype)
        k_keep = k * (1.0 - k_a)
        k_gated = k * k_a
        kd = []
        for direction, (b_out, kd_out, lw_out) in enumerate(
                ((b0_out, kd0_out, lw0_out), (b1_out, kd1_out, lw1_out))):
            sl = slice(direction * d, (direction + 1) * d)
            half_c = -0.5 * math.exp(-0.5)
            lw_out[rows, :] = half_c * jnp.tanh(zw[:, sl]) + half_c
            a_gate = _sigmoid_of_twice(za[:, sl])
            kd.append(k_keep + k_gated * a_gate)
            kd_out[rows, :] = kd[direction].astype(kd_out.dtype)
            b_out[rows, :] = (kk * a_gate).astype(b_out.dtype)
        bonus_out[rows, :] = _head_sum(r * (kd[0] + kd[1]) * rk_ref[...], ones_bd) * v

    sub_rows = [slice(s * sub, (s + 1) * sub) for s in range(tm // sub)]
    projected = [project(rows) for rows in sub_rows]
    for rows, outs in zip(sub_rows, projected):
        epilogue(rows, *outs)


def _rwkv_pre(x2d, seq_len, norm, mu, wr, wk, wv, g1, g2, w1, w2, w0, a1, a2, a0,
              k_scale, k_a, r_k, ones_bd, *, tm, sub):
    m, d = x2d.shape
    n_halo = m // SUBLANES
    halo = tm // SUBLANES
    row_spec = pl.BlockSpec((tm, d), lambda i: (i, 0))

    def full(a):
        return pl.BlockSpec(a.shape, lambda i: (0,) * a.ndim)

    params = (norm, mu, wr, wk, wv, g1, g2, w1, w2, w0, a1, a2, a0, k_scale, k_a, r_k, ones_bd)
    return pl.pallas_call(
        functools.partial(_rwkv_pre_kernel, tm=tm, sub=sub, seq_len=seq_len),
        grid=(m // tm,),
        in_specs=[row_spec,
                  pl.BlockSpec((SUBLANES, d), lambda i: (jnp.maximum(i * halo - 1, 0), 0)),
                  pl.BlockSpec((SUBLANES, d), lambda i: (jnp.minimum((i + 1) * halo, n_halo - 1), 0)),
                  ] + [full(p) for p in params],
        out_specs=[row_spec] * 11,
        out_shape=[jax.ShapeDtypeStruct((m, d), dt)
                   for dt in (BF16, BF16, BF16, F32, F32) + (BF16, BF16, F32) * 2],
        compiler_params=pltpu.CompilerParams(
            dimension_semantics=("arbitrary",), vmem_limit_bytes=VMEM_LIMIT_BYTES),
        name="rwkv_pre",
    )(x2d, x2d, x2d, *params)


def _scan_operands(r_ref, v_ref, kk_ref, b_ref, kd_ref, lw_ref, bi, reverse):
    n = SCAN_CHUNK
    ti = lax.broadcasted_iota(jnp.int32, (n, n), 0)
    tj = lax.broadcasted_iota(jnp.int32, (n, n), 1)
    cum_mat = ((ti <= tj) if reverse else (ti >= tj)).astype(BF16)
    lw = lw_ref[bi]
    hi = lw.astype(BF16)
    lo = (lw - hi.astype(F32)).astype(BF16)
    cum = jnp.dot(jnp.concatenate([cum_mat, cum_mat], axis=1), jnp.concatenate([hi, lo], axis=0),
                  preferred_element_type=F32)
    e_incl = jnp.exp(cum)
    e_inv = jnp.exp(-cum)
    last = 0 if reverse else n - 1
    return dict(
        a=(-(kk_ref[bi].astype(F32) * jnp.exp(cum - lw))).astype(BF16),
        r=(r_ref[bi].astype(F32) * e_incl).astype(BF16),
        b=(b_ref[bi].astype(F32) * e_inv).astype(BF16),
        k=(kd_ref[bi].astype(F32) * e_inv).astype(BF16),
        v=v_ref[bi].astype(BF16),
        chunk_decay=e_incl[last:last + 1, :])


def _wkv_kernel(rf_ref, vf_ref, kkf_ref, bf_ref, kdf_ref, lwf_ref,
                rb_ref, vb_ref, kkb_ref, bb_ref, kdb_ref, lwb_ref,
                of_ref, ob_ref, state_ref):
    @pl.when(pl.program_id(0) == 0)
    def _():
        state_ref[...] = jnp.zeros_like(state_ref)

    n = SCAN_CHUNK
    batch, _, d = rf_ref.shape
    ci = lax.broadcasted_iota(jnp.int32, (n, LANES), 0)
    cj = lax.broadcasted_iota(jnp.int32, (n, LANES), 1)
    head0 = cj < HEAD_DIM
    cj = cj % HEAD_DIM
    same_head = (lax.broadcasted_iota(jnp.int32, (PAIR, LANES), 0) // HEAD_DIM
                 == lax.broadcasted_iota(jnp.int32, (PAIR, LANES), 1) // HEAD_DIM)

    def expand(x):
        zero = jnp.zeros_like(x)
        return jnp.concatenate([jnp.where(head0, x, zero), jnp.where(head0, zero, x)], axis=0)

    ops = [(_scan_operands(rf_ref, vf_ref, kkf_ref, bf_ref, kdf_ref, lwf_ref, bi, False),
            _scan_operands(rb_ref, vb_ref, kkb_ref, bb_ref, kdb_ref, lwb_ref, bi, True))
           for bi in range(batch)]

    def run_stages(chains):
        lanes = [slice(p * LANES, (p + 1) * LANES) for _, _, p in chains]
        strict = [(ci < cj) if rev else (ci > cj) for _, rev, _ in chains]
        incl = [(ci <= cj) if rev else (ci >= cj) for _, rev, _ in chains]
        idx = range(len(chains))

        def tile(name, i):
            bi, direction, _ = chains[i]
            return ops[bi][direction][name][:, lanes[i]]

        ar = [jnp.concatenate([tile("a", i), tile("r", i)], axis=0) for i in idx]
        bk = [jnp.concatenate([tile("b", i), tile("k", i)], axis=0) for i in idx]
        bk_x = [jnp.concatenate([expand(tile("b", i)), expand(tile("k", i))], axis=0) for i in idx]
        v_x = [expand(tile("v", i)) for i in idx]
        state = [state_ref[bi, direction, p] for bi, direction, p in chains]

        gram = [_mm_nt(ar[i], bk_x[i]) for i in idx]
        from_state = [_mm_nt(ar[i], state[i]) for i in idx]
        power = [jnp.where(strict[i], gram[i][:n, :PAIR], 0.0).astype(BF16) for i in idx]
        a_ak = [jnp.where(strict[i], gram[i][:n, PAIR:], 0.0) for i in idx]
        a_r = [jnp.where(jnp.concatenate([incl[i], incl[i]], axis=1), gram[i][n:, :], 0.0).astype(BF16)
               for i in idx]

        u = [from_state[i][:n] + _mm(a_ak[i], v_x[i]) for i in idx]
        steps = int(math.log2(n))
        for s in range(steps - 1):
            both = [jnp.dot(power[i],
                            jnp.concatenate([expand(power[i]), expand(u[i].astype(BF16))], axis=1),
                            preferred_element_type=F32) for i in idx]
            power = [x[:, :PAIR].astype(BF16) for x in both]
            u = [u[i] + both[i][:, PAIR:] for i in idx]
        u = [u[i] + jnp.dot(power[i], expand(u[i].astype(BF16)), preferred_element_type=F32)
             for i in idx]

        u16 = [x.astype(BF16) for x in u]
        out = [from_state[i][n:] + jnp.dot(a_r[i], jnp.concatenate([expand(u16[i]), v_x[i]], axis=0),
                                           preferred_element_type=F32) for i in idx]
        grown = [_mm_tn(jnp.concatenate([u16[i], tile("v", i)], axis=0), bk[i]) for i in idx]
        for i, (bi, direction, p) in enumerate(chains):
            (ob_ref if direction else of_ref)[bi, :, lanes[i]] = out[i]
            decay = ops[bi][direction]["chunk_decay"][:, lanes[i]]
            state_ref[bi, direction, p] = (state[i] + jnp.where(same_head, grown[i], 0.0)) * decay

    run_stages([(bi, direction, p) for bi in range(batch) for direction in range(2)
                for p in range(d // LANES)])


def _wkv_scan(r, v, kk, b0, kd0, lw0, b1, kd1, lw1, batch, seq_len):
    m, d = r.shape
    n_chunks = seq_len // SCAN_CHUNK
    fwd = pl.BlockSpec((batch, SCAN_CHUNK, d), lambda c: (0, c, 0))
    bwd = pl.BlockSpec((batch, SCAN_CHUNK, d), lambda c: (0, n_chunks - 1 - c, 0))
    rows = lambda a: a.reshape(batch, seq_len, d)
    r, v, kk, b0, kd0, lw0, b1, kd1, lw1 = map(rows, (r, v, kk, b0, kd0, lw0, b1, kd1, lw1))
    s0, s1 = pl.pallas_call(
        _wkv_kernel,
        grid=(n_chunks,),
        in_specs=[fwd] * 6 + [bwd] * 6,
        out_specs=[fwd, bwd],
        out_shape=[jax.ShapeDtypeStruct((batch, seq_len, d), F32)] * 2,
        scratch_shapes=[pltpu.VMEM((batch, 2, d // LANES, PAIR, LANES), F32)],
        compiler_params=pltpu.CompilerParams(
            dimension_semantics=("arbitrary",), vmem_limit_bytes=VMEM_LIMIT_BYTES),
        name="wkv_scan",
    )(r, v, kk, b0, kd0, lw0, r, v, kk, b1, kd1, lw1)
    return s0.reshape(m, d), s1.reshape(m, d)


def _ffn_rows(x, norm_ref, w1_ref, w2_ref, th, side):
    xn = _rms(x, norm_ref[...]).astype(BF16)
    n_blocks = w1_ref.shape[1] // th

    def up(j):
        h = jnp.maximum(jnp.dot(xn, w1_ref[:, j * th:(j + 1) * th],
                                preferred_element_type=F32), 0.0)
        side()
        return (h * h).astype(BF16)

    acc = None
    act = up(0)
    for j in range(n_blocks):
        nxt = up(j + 1) if j + 1 < n_blocks else None
        part = jnp.dot(act, w2_ref[j * th:(j + 1) * th, :], preferred_element_type=F32)
        side()
        acc = part if acc is None else acc + part
        act = nxt
    return x + acc


def _rwkv_out_ffn_kernel(x_ref, s0_ref, s1_ref, bonus_ref, g_ref, lng_ref, lnb_ref, ones_ref,
                         wo_ref, norm_ref, w1_ref, w2_ref, o_ref, mixed_ref, *, th):
    step = pl.program_id(0)

    @pl.when(step == 0)
    def _():
        mixed_ref[1] = jnp.zeros(mixed_ref.shape[1:], mixed_ref.dtype)

    def output_stage(slot):
        ones_bd = ones_ref[...]
        keep = {}

        def add_directions():
            keep["s"] = s0_ref[...] + s1_ref[...]

        def head_mean():
            keep["mean"] = _head_sum(keep["s"], ones_bd) * (1.0 / HEAD_DIM)

        def centre():
            keep["oc"] = keep.pop("s") - keep.pop("mean")
            keep["sq"] = keep["oc"] * keep["oc"]

        def head_var():
            keep["var"] = _head_sum(keep.pop("sq"), ones_bd) * (1.0 / HEAD_DIM)

        def normalise():
            o = (keep.pop("oc") * lax.rsqrt(keep.pop("var") + GN_EPS) * lng_ref[...]
                 + lnb_ref[...] + bonus_ref[...])
            keep["gated"] = (o * g_ref[...]).astype(BF16)

        def project():
            mixed_ref[slot] = x_ref[...] + jnp.dot(keep.pop("gated"), wo_ref[...],
                                                   preferred_element_type=F32)

        return [add_directions, head_mean, centre, head_var, normalise, project]

    for prep_slot in range(2):
        @pl.when(step % 2 == prep_slot)
        def _():
            pieces = output_stage(prep_slot)

            def side():
                if pieces:
                    pieces.pop(0)()

            o_ref[...] = _ffn_rows(mixed_ref[1 - prep_slot], norm_ref, w1_ref, w2_ref, th, side)
            while pieces:
                side()


def _layer_weight_spec(stacked, layer):
    return pl.BlockSpec((None,) + stacked.shape[1:], lambda i: (layer, 0, 0),
                        pipeline_mode=pl.Buffered(1))


def _rwkv_out_ffn(x2d, s0, s1, bonus, g, ln_g, ln_b, ones_bd, wo, norm, w1_all, w2_all, layer,
                  *, tm, th):
    m, d = x2d.shape
    last = m // tm - 1
    rows_in = pl.BlockSpec((tm, d), lambda i: (jnp.minimum(i, last), 0))
    rows_out = pl.BlockSpec((tm, d), lambda i: (jnp.maximum(i - 1, 0), 0))

    def full(a):
        return pl.BlockSpec(a.shape, lambda i: (0,) * a.ndim)

    params = (ln_g, ln_b, ones_bd, wo, norm)
    return pl.pallas_call(
        functools.partial(_rwkv_out_ffn_kernel, th=th),
        grid=(m // tm + 1,),
        in_specs=[rows_in] * 5 + [full(p) for p in params] + [
            _layer_weight_spec(w1_all, layer), _layer_weight_spec(w2_all, layer)],
        out_specs=rows_out,
        out_shape=jax.ShapeDtypeStruct((m, d), F32),
        scratch_shapes=[pltpu.VMEM((2, tm, d), F32)],
        compiler_params=pltpu.CompilerParams(
            dimension_semantics=("arbitrary",), vmem_limit_bytes=VMEM_LIMIT_BYTES),
        name="rwkv_out_ffn",
    )(x2d, s0, s1, bonus, g, *params, w1_all, w2_all)


def _ffn_kernel(x_ref, norm_ref, w1_ref, w2_ref, fnorm_ref, o_ref, *, th, final_norm):
    y = _ffn_rows(x_ref[...], norm_ref, w1_ref, w2_ref, th, lambda: None)
    o_ref[...] = _rms(y, fnorm_ref[...]) if final_norm else y


def _ffn(x2d, norm, w1_all, w2_all, layer, fnorm, *, tm, th, final_norm):
    m, d = x2d.shape

    def full(a):
        return pl.BlockSpec(a.shape, lambda i: (0,) * a.ndim)

    return pl.pallas_call(
        functools.partial(_ffn_kernel, th=th, final_norm=final_norm),
        grid=(m // tm,),
        in_specs=[pl.BlockSpec((tm, d), lambda i: (i, 0)),
                  full(norm), _layer_weight_spec(w1_all, layer),
                  _layer_weight_spec(w2_all, layer), full(fnorm)],
        out_specs=pl.BlockSpec((tm, d), lambda i: (i, 0)),
        out_shape=jax.ShapeDtypeStruct((m, d), F32),
        compiler_params=pltpu.CompilerParams(
            dimension_semantics=("arbitrary",), vmem_limit_bytes=VMEM_LIMIT_BYTES),
        name="ffn_final" if final_norm else "ffn",
    )(x2d, norm, w1_all, w2_all, fnorm)


def _gmlp_kernel(x_ref, norm_ref, win_ref, lng_ref, lnb_ref, ws_ref, bs_ref, wout_ref,
                 o_ref, gated_ref, *, tm, sub):
    n_sub = tm // sub
    n_chunks = sub // GMLP_CHUNK
    width = wout_ref.shape[0]
    hidden = [None] * n_sub

    def in_proj(i):
        xn = _rms(x_ref[i * sub:(i + 1) * sub, :], norm_ref[...]).astype(BF16)
        hidden[i] = jnp.dot(xn, win_ref[...], preferred_element_type=F32)

    def mix(i):
        h = hidden[i]
        h = 0.5 * h * (1.0 + lax.erf(h * (1.0 / math.sqrt(2.0))))
        u = h[:, :width]
        v = h[:, width:]
        mean = jnp.mean(v, axis=-1, keepdims=True)
        vc = v - mean
        var = jnp.mean(vc * vc, axis=-1, keepdims=True)
        v = (vc * lax.rsqrt(var + NORM_EPS) * lng_ref[...] + lnb_ref[...]).astype(BF16)
        for grp in range(width // GMLP_GROUP):
            cols = slice(grp * GMLP_GROUP, (grp + 1) * GMLP_GROUP)
            stacked = jnp.concatenate(
                [v[c * GMLP_CHUNK:(c + 1) * GMLP_CHUNK, cols] for c in range(n_chunks)], axis=1)
            mixed = jnp.dot(ws_ref[grp], stacked, preferred_element_type=F32)
            for c in range(n_chunks):
                rows = slice(c * GMLP_CHUNK, (c + 1) * GMLP_CHUNK)
                part = mixed[:, c * GMLP_GROUP:(c + 1) * GMLP_GROUP] + bs_ref[:, cols]
                gated_ref[i, rows, cols] = (u[rows, cols] * part).astype(BF16)

    def out_proj(i):
        rows = slice(i * sub, (i + 1) * sub)
        o_ref[rows, :] = x_ref[rows, :] + jnp.dot(gated_ref[i], wout_ref[...],
                                                  preferred_element_type=F32)

    in_proj(0)
    for i in range(n_sub):
        if i + 1 < n_sub:
            in_proj(i + 1)
        mix(i)
        if i >= 1:
            out_proj(i - 1)
    out_proj(n_sub - 1)


def _gmlp(x2d, norm, w_in, ln_g, ln_b, w_s, b_s_full, w_out, *, tm, sub):
    m, d = x2d.shape
    width = w_out.shape[0]
    row_spec = pl.BlockSpec((tm, d), lambda i: (i, 0))

    def full(a):
        return pl.BlockSpec(a.shape, lambda i: (0,) * a.ndim)

    params = (norm, w_in, ln_g, ln_b, w_s, b_s_full, w_out)
    return pl.pallas_call(
        functools.partial(_gmlp_kernel, tm=tm, sub=sub),
        grid=(m // tm,),
        in_specs=[row_spec] + [full(p) for p in params],
        out_specs=row_spec,
        out_shape=jax.ShapeDtypeStruct((m, d), F32),
        scratch_shapes=[pltpu.VMEM((tm // sub, sub, width), BF16)],
        compiler_params=pltpu.CompilerParams(
            dimension_semantics=("arbitrary",), vmem_limit_bytes=VMEM_LIMIT_BYTES),
        name="gmlp",
    )(x2d, *params)


def _block_diag2(w):
    z = jnp.zeros_like(w[0])
    return jnp.concatenate([jnp.concatenate([w[0], z], axis=1),
                            jnp.concatenate([z, w[1]], axis=1)], axis=0)


def kernel(x, mix_norm, ffn_norm, final_norm, rwkv_mu, rwkv_wr, rwkv_wk, rwkv_wv, rwkv_wo, rwkv_w0, rwkv_w1, rwkv_w2, rwkv_a0, rwkv_a1, rwkv_a2, rwkv_g1, rwkv_g2, rwkv_k_k, rwkv_k_a, rwkv_r_k, rwkv_ln_g, rwkv_ln_b, gmlp_w_in, gmlp_ln_g, gmlp_ln_b, gmlp_w_s, gmlp_b_s, gmlp_w_out, ffn_w1, ffn_w2):
    batch, seq_len, d = x.shape
    depth = mix_norm.shape[0]
    m = batch * seq_len
    assert d % LANES == 0 and seq_len % SCAN_CHUNK == 0 and seq_len % GMLP_CHUNK == 0
    tm_pre = min(512, seq_len)
    sub_pre = min(256, tm_pre)
    tm_post = min(512, seq_len)
    tm_ffn = min(1024, m)
    tm_gmlp = min(1024, seq_len)
    sub_gmlp = min(256, tm_gmlp)
    th_ffn = 1024

    lane = jnp.arange(MXU_TILE) // HEAD_DIM
    ones_bd = (lane[:, None] == lane[None, :]).astype(BF16)
    row = lambda a: a.reshape(1, -1).astype(F32)
    cat_dirs = lambda w: jnp.concatenate([w[0], w[1]], axis=-1)

    ffn_w1_all = ffn_w1.astype(BF16)
    ffn_w2_all = ffn_w2.astype(BF16)
    h = x.reshape(m, d)
    for layer in range(depth):
        j = layer // 2
        norm = row(mix_norm[layer])
        if layer % 2 == 0:
            outs = _rwkv_pre(
                h, seq_len, norm, rwkv_mu[j],
                rwkv_wr[j].astype(BF16), rwkv_wk[j].astype(BF16), rwkv_wv[j].astype(BF16),
                (0.5 * rwkv_g1[j]).astype(BF16), rwkv_g2[j].astype(BF16),
                cat_dirs(rwkv_w1[j]).astype(BF16), _block_diag2(0.5 * rwkv_w2[j]).astype(BF16),
                row(cat_dirs(0.5 * rwkv_w0[j])),
                cat_dirs(rwkv_a1[j]).astype(BF16), _block_diag2(0.5 * rwkv_a2[j]).astype(BF16),
                row(cat_dirs(0.5 * rwkv_a0[j])),
                row(rwkv_k_k[j]), row(rwkv_k_a[j]), row(rwkv_r_k[j]), ones_bd, tm=tm_pre,
                sub=sub_pre)
            r, v, kk, g, bonus, b0, kd0, lw0, b1, kd1, lw1 = outs
            s0, s1 = _wkv_scan(r, v, kk, b0, kd0, lw0, b1, kd1, lw1, batch, seq_len)
            assert layer < depth - 1
            h = _rwkv_out_ffn(h, s0, s1, bonus, g, row(rwkv_ln_g[j]), row(rwkv_ln_b[j]), ones_bd,
                              rwkv_wo[j].astype(BF16), row(ffn_norm[layer]), ffn_w1_all, ffn_w2_all,
                              layer, tm=tm_post, th=th_ffn)
            continue
        else:
            b_full = jnp.repeat(jnp.transpose(gmlp_b_s[j]), GMLP_GROUP, axis=1)
            h = _gmlp(h, norm, gmlp_w_in[j].astype(BF16), row(gmlp_ln_g[j]), row(gmlp_ln_b[j]),
                      gmlp_w_s[j].astype(BF16), b_full, gmlp_w_out[j].astype(BF16), tm=tm_gmlp,
                      sub=sub_gmlp)
        last = layer == depth - 1
        h = _ffn(h, row(ffn_norm[layer]), ffn_w1_all, ffn_w2_all, layer, row(final_norm),
                 tm=tm_ffn, th=th_ffn, final_norm=last)
    return h.reshape(batch, seq_len, d)
```

```python
import functools
import math

import jax
import jax.numpy as jnp
from jax import lax
from jax.experimental import pallas as pl
from jax.experimental.pallas import tpu as pltpu

F32 = jnp.float32
BF16 = jnp.bfloat16

HEAD_DIM = 64
GN_EPS = 64e-5
NORM_EPS = 1e-5
GMLP_CHUNK = 128
GMLP_GROUP = 128

LANES = 128
SUBLANES = 8
MXU_TILE = 256
HEADS_PER_TILE = LANES // HEAD_DIM
VMEM_LIMIT_BYTES = 56 * 1024 * 1024

SCAN_CHUNK = 64
PAIR = HEADS_PER_TILE * SCAN_CHUNK


def _mm(a, b):
    return jnp.dot(a.astype(BF16), b.astype(BF16), preferred_element_type=F32)


def _mm_nt(a, b):
    return lax.dot_general(a.astype(BF16), b.astype(BF16), (((1,), (1,)), ((), ())),
                           preferred_element_type=F32)


def _mm_tn(a, b):
    return lax.dot_general(a.astype(BF16), b.astype(BF16), (((0,), (0,)), ((), ())),
                           preferred_element_type=F32)


def _sigmoid_of_twice(half_x):
    return 0.5 * jnp.tanh(half_x) + 0.5


def _rms(x, g):
    return x * lax.rsqrt(jnp.mean(x * x, axis=-1, keepdims=True) + NORM_EPS) * g


def _head_sum(x, ones_bd):
    width = ones_bd.shape[0]
    xb = x.astype(BF16)
    cols = [jnp.dot(xb[:, j * width:(j + 1) * width], ones_bd, preferred_element_type=F32)
            for j in range(x.shape[1] // width)]
    return jnp.concatenate(cols, axis=1)


def _rwkv_pre_kernel(x_ref, xprev_ref, xnext_ref, norm_ref, mu_ref,
                     wr_ref, wk_ref, wv_ref, g1_ref, g2_ref,
                     w1_ref, w2_ref, w0_ref, a1_ref, a2_ref, a0_ref,
                     kscale_ref, ka_ref, rk_ref, ones_ref,
                     r_out, v_out, kk_out, g_out, bonus_out,
                     b0_out, kd0_out, lw0_out, b1_out, kd1_out, lw1_out,
                     *, tm, sub, seq_len):
    i = pl.program_id(0)
    d = x_ref.shape[1]
    gain = norm_ref[...]
    xn = _rms(x_ref[...], gain)
    at_start = (i * tm) % seq_len == 0
    at_end = ((i + 1) * tm) % seq_len == 0
    edge_prev = jnp.where(at_start, 0.0, _rms(xprev_ref[SUBLANES - 1:SUBLANES, :], gain))
    edge_next = jnp.where(at_end, 0.0, _rms(xnext_ref[0:1, :], gain))
    row = lax.broadcasted_iota(jnp.int32, (tm, 1), 0)
    x_prev = jnp.where(row == 0, edge_prev, pltpu.roll(xn, 1, axis=0))
    x_next = jnp.where(row == tm - 1, edge_next, pltpu.roll(xn, tm - 1, axis=0))
    xx = 0.5 * (x_prev + x_next) - xn
    ones_bd = ones_ref[...]
    k_a = ka_ref[...]

    def project(rows):
        def mixed(j):
            return (xn[rows] + xx[rows] * mu_ref[j:j + 1, :]).astype(BF16)

        dot = functools.partial(jnp.dot, preferred_element_type=F32)
        w_mid = dot(mixed(1), w1_ref[...])
        a_mid = dot(mixed(4), a1_ref[...])
        g_mid = dot(mixed(5), g1_ref[...])
        zw = _mm(jnp.tanh(w_mid), w2_ref[...]) + w0_ref[...]
        za = _mm(a_mid, a2_ref[...]) + a0_ref[...]
        g_out[rows, :] = _mm(_sigmoid_of_twice(g_mid), g2_ref[...])
        k = dot(mixed(2), wk_ref[...])
        r = dot(mixed(0), wr_ref[...])
        v = dot(mixed(3), wv_ref[...])
        return zw, za, k, r, v

    def epilogue(rows, zw, za, k, r, v):
        r_out[rows, :] = r.astype(r_out.dtype)
        v_out[rows, :] = v.astype(v_out.dtype)
        kk = k * kscale_ref[...]
        kk = kk * lax.rsqrt(jnp.maximum(_head_sum(kk * kk, ones_bd), 1e-24))
        kk_out[rows, :] = kk.astype(kk_out.dtype)
        k_keep = k * (1.0 - k_a)
        k_gated = k * k_a
        kd = []
        for direction, (b_out, kd_out, lw_out) in enumerate(
                ((b0_out, kd0_out, lw0_out), (b1_out, kd1_out, lw1_out))):
            sl = slice(direction * d, (direction + 1) * d)
            half_c = -0.5 * math.exp(-0.5)
            lw_out[rows, :] = half_c * jnp.tanh(zw[:, sl]) + half_c
            a_gate = _sigmoid_of_twice(za[:, sl])
            kd.append(k_keep + k_gated * a_gate)
            kd_out[rows, :] = kd[direction].astype(kd_out.dtype)
            b_out[rows, :] = (kk * a_gate).astype(b_out.dtype)
        bonus_out[rows, :] = _head_sum(r * (kd[0] + kd[1]) * rk_ref[...], ones_bd) * v

    sub_rows = [slice(s * sub, (s + 1) * sub) for s in range(tm // sub)]
    projected = [project(rows) for rows in sub_rows]
    for rows, outs in zip(sub_rows, projected):
        epilogue(rows, *outs)


def _rwkv_pre(x2d, seq_len, norm, mu, wr, wk, wv, g1, g2, w1, w2, w0, a1, a2, a0,
              k_scale, k_a, r_k, ones_bd, *, tm, sub):
    m, d = x2d.shape
    n_halo = m // SUBLANES
    halo = tm // SUBLANES
    row_spec = pl.BlockSpec((tm, d), lambda i: (i, 0))

    def full(a):
        return pl.BlockSpec(a.shape, lambda i: (0,) * a.ndim)

    params = (norm, mu, wr, wk, wv, g1, g2, w1, w2, w0, a1, a2, a0, k_scale, k_a, r_k, ones_bd)
    return pl.pallas_call(
        functools.partial(_rwkv_pre_kernel, tm=tm, sub=sub, seq_len=seq_len),
        grid=(m // tm,),
        in_specs=[row_spec,
                  pl.BlockSpec((SUBLANES, d), lambda i: (jnp.maximum(i * halo - 1, 0), 0)),
                  pl.BlockSpec((SUBLANES, d), lambda i: (jnp.minimum((i + 1) * halo, n_halo - 1), 0)),
                  ] + [full(p) for p in params],
        out_specs=[row_spec] * 11,
        out_shape=[jax.ShapeDtypeStruct((m, d), dt)
                   for dt in (BF16, BF16, BF16, F32, F32) + (BF16, BF16, F32) * 2],
        compiler_params=pltpu.CompilerParams(
            dimension_semantics=("arbitrary",), vmem_limit_bytes=VMEM_LIMIT_BYTES),
        name="rwkv_pre",
    )(x2d, x2d, x2d, *params)


def _scan_operands(r_ref, v_ref, kk_ref, b_ref, kd_ref, lw_ref, bi, reverse):
    n = SCAN_CHUNK
    ti = lax.broadcasted_iota(jnp.int32, (n, n), 0)
    tj = lax.broadcasted_iota(jnp.int32, (n, n), 1)
    cum_mat = ((ti <= tj) if reverse else (ti >= tj)).astype(BF16)
    lw = lw_ref[bi]
    hi = lw.astype(BF16)
    lo = (lw - hi.astype(F32)).astype(BF16)
    cum = jnp.dot(jnp.concatenate([cum_mat, cum_mat], axis=1), jnp.concatenate([hi, lo], axis=0),
                  preferred_element_type=F32)
    e_incl = jnp.exp(cum)
    e_inv = jnp.exp(-cum)
    last = 0 if reverse else n - 1
    return dict(
        a=(-(kk_ref[bi].astype(F32) * jnp.exp(cum - lw))).astype(BF16),
        r=(r_ref[bi].astype(F32) * e_incl).astype(BF16),
        b=(b_ref[bi].astype(F32) * e_inv).astype(BF16),
        k=(kd_ref[bi].astype(F32) * e_inv).astype(BF16),
        v=v_ref[bi].astype(BF16),
        chunk_decay=e_incl[last:last + 1, :])


def _wkv_kernel(rf_ref, vf_ref, kkf_ref, bf_ref, kdf_ref, lwf_ref,
                rb_ref, vb_ref, kkb_ref, bb_ref, kdb_ref, lwb_ref,
                of_ref, ob_ref, state_ref):
    @pl.when(pl.program_id(0) == 0)
    def _():
        state_ref[...] = jnp.zeros_like(state_ref)

    n = SCAN_CHUNK
    batch, _, d = rf_ref.shape
    ci = lax.broadcasted_iota(jnp.int32, (n, LANES), 0)
    cj = lax.broadcasted_iota(jnp.int32, (n, LANES), 1)
    head0 = cj < HEAD_DIM
    cj = cj % HEAD_DIM
    same_head = (lax.broadcasted_iota(jnp.int32, (PAIR, LANES), 0) // HEAD_DIM
                 == lax.broadcasted_iota(jnp.int32, (PAIR, LANES), 1) // HEAD_DIM)

    def expand(x):
        zero = jnp.zeros_like(x)
        return jnp.concatenate([jnp.where(head0, x, zero), jnp.where(head0, zero, x)], axis=0)

    ops = [(_scan_operands(rf_ref, vf_ref, kkf_ref, bf_ref, kdf_ref, lwf_ref, bi, False),
            _scan_operands(rb_ref, vb_ref, kkb_ref, bb_ref, kdb_ref, lwb_ref, bi, True))
           for bi in range(batch)]

    def run_stages(chains):
        lanes = [slice(p * LANES, (p + 1) * LANES) for _, _, p in chains]
        strict = [(ci < cj) if rev else (ci > cj) for _, rev, _ in chains]
        incl = [(ci <= cj) if rev else (ci >= cj) for _, rev, _ in chains]
        idx = range(len(chains))

        def tile(name, i):
            bi, direction, _ = chains[i]
            return ops[bi][direction][name][:, lanes[i]]

        ar = [jnp.concatenate([tile("a", i), tile("r", i)], axis=0) for i in idx]
        bk = [jnp.concatenate([tile("b", i), tile("k", i)], axis=0) for i in idx]
        bk_x = [jnp.concatenate([expand(tile("b", i)), expand(tile("k", i))], axis=0) for i in idx]
        v_x = [expand(tile("v", i)) for i in idx]
        state = [state_ref[bi, direction, p] for bi, direction, p in chains]

        gram = [_mm_nt(ar[i], bk_x[i]) for i in idx]
        from_state = [_mm_nt(ar[i], state[i]) for i in idx]
        power = [jnp.where(strict[i], gram[i][:n, :PAIR], 0.0).astype(BF16) for i in idx]
        a_ak = [jnp.where(strict[i], gram[i][:n, PAIR:], 0.0) for i in idx]
        a_r = [jnp.where(jnp.concatenate([incl[i], incl[i]], axis=1), gram[i][n:, :], 0.0).astype(BF16)
               for i in idx]

        u = [from_state[i][:n] + _mm(a_ak[i], v_x[i]) for i in idx]
        steps = int(math.log2(n))
        for s in range(steps - 1):
            both = [jnp.dot(power[i],
                            jnp.concatenate([expand(power[i]), expand(u[i].astype(BF16))], axis=1),
                            preferred_element_type=F32) for i in idx]
            power = [x[:, :PAIR].astype(BF16) for x in both]
            u = [u[i] + both[i][:, PAIR:] for i in idx]
        u = [u[i] + jnp.dot(power[i], expand(u[i].astype(BF16)), preferred_element_type=F32)
             for i in idx]

        u16 = [x.astype(BF16) for x in u]
        out = [from_state[i][n:] + jnp.dot(a_r[i], jnp.concatenate([expand(u16[i]), v_x[i]], axis=0),
                                           preferred_element_type=F32) for i in idx]
        grown = [_mm_tn(jnp.concatenate([u16[i], tile("v", i)], axis=0), bk[i]) for i in idx]
        for i, (bi, direction, p) in enumerate(chains):
            (ob_ref if direction else of_ref)[bi, :, lanes[i]] = out[i]
            decay = ops[bi][direction]["chunk_decay"][:, lanes[i]]
            state_ref[bi, direction, p] = (state[i] + jnp.where(same_head, grown[i], 0.0)) * decay

    run_stages([(bi, direction, p) for bi in range(batch) for direction in range(2)
                for p in range(d // LANES)])


def _wkv_scan(r, v, kk, b0, kd0, lw0, b1, kd1, lw1, batch, seq_len):
    m, d = r.shape
    n_chunks = seq_len // SCAN_CHUNK
    fwd = pl.BlockSpec((batch, SCAN_CHUNK, d), lambda c: (0, c, 0))
    bwd = pl.BlockSpec((batch, SCAN_CHUNK, d), lambda c: (0, n_chunks - 1 - c, 0))
    rows = lambda a: a.reshape(batch, seq_len, d)
    r, v, kk, b0, kd0, lw0, b1, kd1, lw1 = map(rows, (r, v, kk, b0, kd0, lw0, b1, kd1, lw1))
    s0, s1 = pl.pallas_call(
        _wkv_kernel,
        grid=(n_chunks,),
        in_specs=[fwd] * 6 + [bwd] * 6,
        out_specs=[fwd, bwd],
        out_shape=[jax.ShapeDtypeStruct((batch, seq_len, d), F32)] * 2,
        scratch_shapes=[pltpu.VMEM((batch, 2, d // LANES, PAIR, LANES), F32)],
        compiler_params=pltpu.CompilerParams(
            dimension_semantics=("arbitrary",), vmem_limit_bytes=VMEM_LIMIT_BYTES),
        name="wkv_scan",
    )(r, v, kk, b0, kd0, lw0, r, v, kk, b1, kd1, lw1)
    return s0.reshape(m, d), s1.reshape(m, d)


def _ffn_rows(x, norm_ref, w1_ref, w2_ref, th, side):
    xn = _rms(x, norm_ref[...]).astype(BF16)
    n_blocks = w1_ref.shape[1] // th

    def up(j):
        h = jnp.maximum(jnp.dot(xn, w1_ref[:, j * th:(j + 1) * th],
                                preferred_element_type=F32), 0.0)
        side()
        return (h * h).astype(BF16)

    acc = None
    act = up(0)
    for j in range(n_blocks):
        nxt = up(j + 1) if j + 1 < n_blocks else None
        part = jnp.dot(act, w2_ref[j * th:(j + 1) * th, :], preferred_element_type=F32)
        side()
        acc = part if acc is None else acc + part
        act = nxt
    return x + acc


def _rwkv_out_ffn_kernel(x_ref, s0_ref, s1_ref, bonus_ref, g_ref, lng_ref, lnb_ref, ones_ref,
                         wo_ref, norm_ref, w1_ref, w2_ref, o_ref, mixed_ref, *, th, n_tiles):
    step = pl.program_id(0)

    def output_stage(slot):
        ones_bd = ones_ref[...]
        keep = {}

        def add_directions():
            keep["s"] = s0_ref[...] + s1_ref[...]

        def head_mean():
            keep["mean"] = _head_sum(keep["s"], ones_bd) * (1.0 / HEAD_DIM)

        def centre():
            keep["oc"] = keep.pop("s") - keep.pop("mean")
            keep["sq"] = keep["oc"] * keep["oc"]

        def head_var():
            keep["var"] = _head_sum(keep.pop("sq"), ones_bd) * (1.0 / HEAD_DIM)

        def normalise():
            o = (keep.pop("oc") * lax.rsqrt(keep.pop("var") + GN_EPS) * lng_ref[...]
                 + lnb_ref[...] + bonus_ref[...])
            keep["gated"] = (o * g_ref[...]).astype(BF16)

        def project():
            mixed_ref[slot] = x_ref[...] + jnp.dot(keep.pop("gated"), wo_ref[...],
                                                   preferred_element_type=F32)

        return [add_directions, head_mean, centre, head_var, normalise, project]

    @pl.when(step == 0)
    def _():
        for piece in output_stage(0):
            piece()

    for prep_slot in range(2):
        @pl.when((step % 2 == prep_slot) & (step > 0) & (step < n_tiles))
        def _():
            pieces = output_stage(prep_slot)

            def side():
                if pieces:
                    pieces.pop(0)()

            o_ref[...] = _ffn_rows(mixed_ref[1 - prep_slot], norm_ref, w1_ref, w2_ref, th, side)
            while pieces:
                side()

    @pl.when(step == n_tiles)
    def _():
        o_ref[...] = _ffn_rows(mixed_ref[(n_tiles - 1) % 2], norm_ref, w1_ref, w2_ref, th,
                               lambda: None)


def _layer_weight_spec(stacked, layer):
    return pl.BlockSpec((None,) + stacked.shape[1:], lambda i: (layer, 0, 0),
                        pipeline_mode=pl.Buffered(1))


def _rwkv_out_ffn(x2d, s0, s1, bonus, g, ln_g, ln_b, ones_bd, wo, norm, w1_all, w2_all, layer,
                  *, tm, th):
    m, d = x2d.shape
    last = m // tm - 1
    rows_in = pl.BlockSpec((tm, d), lambda i: (jnp.minimum(i, last), 0))
    rows_out = pl.BlockSpec((tm, d), lambda i: (jnp.maximum(i - 1, 0), 0))

    def full(a):
        return pl.BlockSpec(a.shape, lambda i: (0,) * a.ndim)

    params = (ln_g, ln_b, ones_bd, wo, norm)
    return pl.pallas_call(
        functools.partial(_rwkv_out_ffn_kernel, th=th, n_tiles=m // tm),
        grid=(m // tm + 1,),
        in_specs=[rows_in] * 5 + [full(p) for p in params] + [
            _layer_weight_spec(w1_all, layer), _layer_weight_spec(w2_all, layer)],
        out_specs=rows_out,
        out_shape=jax.ShapeDtypeStruct((m, d), F32),
        scratch_shapes=[pltpu.VMEM((2, tm, d), F32)],
        compiler_params=pltpu.CompilerParams(
            dimension_semantics=("arbitrary",), vmem_limit_bytes=VMEM_LIMIT_BYTES),
        name="rwkv_out_ffn",
    )(x2d, s0, s1, bonus, g, *params, w1_all, w2_all)


def _ffn_kernel(x_ref, norm_ref, w1_ref, w2_ref, fnorm_ref, o_ref, *, th, final_norm):
    y = _ffn_rows(x_ref[...], norm_ref, w1_ref, w2_ref, th, lambda: None)
    o_ref[...] = _rms(y, fnorm_ref[...]) if final_norm else y


def _ffn(x2d, norm, w1_all, w2_all, layer, fnorm, *, tm, th, final_norm):
    m, d = x2d.shape

    def full(a):
        return pl.BlockSpec(a.shape, lambda i: (0,) * a.ndim)

    return pl.pallas_call(
        functools.partial(_ffn_kernel, th=th, final_norm=final_norm),
        grid=(m // tm,),
        in_specs=[pl.BlockSpec((tm, d), lambda i: (i, 0)),
                  full(norm), _layer_weight_spec(w1_all, layer),
                  _layer_weight_spec(w2_all, layer), full(fnorm)],
        out_specs=pl.BlockSpec((tm, d), lambda i: (i, 0)),
        out_shape=jax.ShapeDtypeStruct((m, d), F32),
        compiler_params=pltpu.CompilerParams(
            dimension_semantics=("arbitrary",), vmem_limit_bytes=VMEM_LIMIT_BYTES),
        name="ffn_final" if final_norm else "ffn",
    )(x2d, norm, w1_all, w2_all, fnorm)


def _gmlp_kernel(x_ref, norm_ref, win_ref, lng_ref, lnb_ref, ws_ref, bs_ref, wout_ref,
                 o_ref, gated_ref, *, tm, sub):
    n_sub = tm // sub
    n_chunks = sub // GMLP_CHUNK
    width = wout_ref.shape[0]
    hidden = [None] * n_sub

    def in_proj(i):
        xn = _rms(x_ref[i * sub:(i + 1) * sub, :], norm_ref[...]).astype(BF16)
        hidden[i] = jnp.dot(xn, win_ref[...], preferred_element_type=F32)

    def mix(i):
        h = hidden[i]
        h = 0.5 * h * (1.0 + lax.erf(h * (1.0 / math.sqrt(2.0))))
        u = h[:, :width]
        v = h[:, width:]
        mean = jnp.mean(v, axis=-1, keepdims=True)
        vc = v - mean
        var = jnp.mean(vc * vc, axis=-1, keepdims=True)
        v = (vc * lax.rsqrt(var + NORM_EPS) * lng_ref[...] + lnb_ref[...]).astype(BF16)
        for grp in range(width // GMLP_GROUP):
            cols = slice(grp * GMLP_GROUP, (grp + 1) * GMLP_GROUP)
            stacked = jnp.concatenate(
                [v[c * GMLP_CHUNK:(c + 1) * GMLP_CHUNK, cols] for c in range(n_chunks)], axis=1)
            mixed = jnp.dot(ws_ref[grp], stacked, preferred_element_type=F32)
            for c in range(n_chunks):
                rows = slice(c * GMLP_CHUNK, (c + 1) * GMLP_CHUNK)
                part = mixed[:, c * GMLP_GROUP:(c + 1) * GMLP_GROUP] + bs_ref[:, cols]
                gated_ref[i, rows, cols] = (u[rows, cols] * part).astype(BF16)

    def out_proj(i):
        rows = slice(i * sub, (i + 1) * sub)
        o_ref[rows, :] = x_ref[rows, :] + jnp.dot(gated_ref[i], wout_ref[...],
                                                  preferred_element_type=F32)

    in_proj(0)
    for i in range(n_sub):
        if i + 1 < n_sub:
            in_proj(i + 1)
        mix(i)
        if i >= 1:
            out_proj(i - 1)
    out_proj(n_sub - 1)


def _gmlp(x2d, norm, w_in, ln_g, ln_b, w_s, b_s_full, w_out, *, tm, sub):
    m, d = x2d.shape
    width = w_out.shape[0]
    row_spec = pl.BlockSpec((tm, d), lambda i: (i, 0))

    def full(a):
        return pl.BlockSpec(a.shape, lambda i: (0,) * a.ndim)

    params = (norm, w_in, ln_g, ln_b, w_s, b_s_full, w_out)
    return pl.pallas_call(
        functools.partial(_gmlp_kernel, tm=tm, sub=sub),
        grid=(m // tm,),
        in_specs=[row_spec] + [full(p) for p in params],
        out_specs=row_spec,
        out_shape=jax.ShapeDtypeStruct((m, d), F32),
        scratch_shapes=[pltpu.VMEM((tm // sub, sub, width), BF16)],
        compiler_params=pltpu.CompilerParams(
            dimension_semantics=("arbitrary",), vmem_limit_bytes=VMEM_LIMIT_BYTES),
        name="gmlp",
    )(x2d, *params)


def _block_diag2(w):
    z = jnp.zeros_like(w[0])
    return jnp.concatenate([jnp.concatenate([w[0], z], axis=1),
                            jnp.concatenate([z, w[1]], axis=1)], axis=0)


def kernel(x, mix_norm, ffn_norm, final_norm, rwkv_mu, rwkv_wr, rwkv_wk, rwkv_wv, rwkv_wo, rwkv_w0, rwkv_w1, rwkv_w2, rwkv_a0, rwkv_a1, rwkv_a2, rwkv_g1, rwkv_g2, rwkv_k_k, rwkv_k_a, rwkv_r_k, rwkv_ln_g, rwkv_ln_b, gmlp_w_in, gmlp_ln_g, gmlp_ln_b, gmlp_w_s, gmlp_b_s, gmlp_w_out, ffn_w1, ffn_w2):
    batch, seq_len, d = x.shape
    depth = mix_norm.shape[0]
    m = batch * seq_len
    assert d % LANES == 0 and seq_len % SCAN_CHUNK == 0 and seq_len % GMLP_CHUNK == 0
    tm_pre = min(512, seq_len)
    sub_pre = min(256, tm_pre)
    tm_post = min(512, seq_len)
    tm_ffn = min(1024, m)
    tm_gmlp = min(1024, seq_len)
    sub_gmlp = min(256, tm_gmlp)
    th_ffn = 1024

    lane = jnp.arange(MXU_TILE) // HEAD_DIM
    ones_bd = (lane[:, None] == lane[None, :]).astype(BF16)
    row = lambda a: a.reshape(1, -1).astype(F32)
    cat_dirs = lambda w: jnp.concatenate([w[0], w[1]], axis=-1)

    ffn_w1_all = ffn_w1.astype(BF16)
    ffn_w2_all = ffn_w2.astype(BF16)
    h = x.reshape(m, d)
    for layer in range(depth):
        j = layer // 2
        norm = row(mix_norm[layer])
        if layer % 2 == 0:
            outs = _rwkv_pre(
                h, seq_len, norm, rwkv_mu[j],
                rwkv_wr[j].astype(BF16), rwkv_wk[j].astype(BF16), rwkv_wv[j].astype(BF16),
                (0.5 * rwkv_g1[j]).astype(BF16), rwkv_g2[j].astype(BF16),
                cat_dirs(rwkv_w1[j]).astype(BF16), _block_diag2(0.5 * rwkv_w2[j]).astype(BF16),
                row(cat_dirs(0.5 * rwkv_w0[j])),
                cat_dirs(rwkv_a1[j]).astype(BF16), _block_diag2(0.5 * rwkv_a2[j]).astype(BF16),
                row(cat_dirs(0.5 * rwkv_a0[j])),
                row(rwkv_k_k[j]), row(rwkv_k_a[j]), row(rwkv_r_k[j]), ones_bd, tm=tm_pre,
                sub=sub_pre)
            r, v, kk, g, bonus, b0, kd0, lw0, b1, kd1, lw1 = outs
            s0, s1 = _wkv_scan(r, v, kk, b0, kd0, lw0, b1, kd1, lw1, batch, seq_len)
            assert layer < depth - 1
            h = _rwkv_out_ffn(h, s0, s1, bonus, g, row(rwkv_ln_g[j]), row(rwkv_ln_b[j]), ones_bd,
                              rwkv_wo[j].astype(BF16), row(ffn_norm[layer]), ffn_w1_all, ffn_w2_all,
                              layer, tm=tm_post, th=th_ffn)
            continue
        else:
            b_full = jnp.repeat(jnp.transpose(gmlp_b_s[j]), GMLP_GROUP, axis=1)
            h = _gmlp(h, norm, gmlp_w_in[j].astype(BF16), row(gmlp_ln_g[j]), row(gmlp_ln_b[j]),
                      gmlp_w_s[j].astype(BF16), b_full, gmlp_w_out[j].astype(BF16), tm=tm_gmlp,
                      sub=sub_gmlp)
        last = layer == depth - 1
        h = _ffn(h, row(ffn_norm[layer]), ffn_w1_all, ffn_w2_all, layer, row(final_norm),
                 tm=tm_ffn, th=th_ffn, final_norm=last)
    return h.reshape(batch, seq_len, d)
```
